```python
import jax
import jax.numpy as jnp
from jax import lax
import numpy as np

D_MODEL = 1024
BATCH = 2
SEQ = 16384
DEPTH = 4

CTX_LEN = 256
GRID_W = 64
EPS = 1e-6
NEG_INF = -1e30
ROPE_BASE = 10000.0
CHUNK = 64

GLA_HEADS = 4
GLA_DK = 64
GLA_DV = 128
GLA_RANK = 16
GLA_TEMP = 16.0
LRU_WIDTH = 512
LRU_BLOCKS = 4
LRU_CONV = 4
LRU_C = 8.0
HEAD_DIM = 64
SWA_QH = 8
SWA_KVH = 2
WINDOW = 128
BLOCK = 128
RET_HEADS = 4
RET_DK = 64
RET_DV = 128

EVEN_SIZES = (GLA_HEADS * GLA_DK, GLA_HEADS * GLA_DK, GLA_HEADS * GLA_DV, GLA_HEADS * GLA_DV,
              2 * GLA_RANK, LRU_WIDTH, LRU_WIDTH)
ODD_SIZES = (SWA_QH * HEAD_DIM, SWA_KVH * HEAD_DIM, SWA_KVH * HEAD_DIM, SWA_QH * HEAD_DIM,
             RET_HEADS * RET_DK, RET_HEADS * RET_DK, RET_HEADS * RET_DV, RET_HEADS * RET_DV)
EVEN_IN = sum(EVEN_SIZES)
ODD_IN = sum(ODD_SIZES)
EVEN_MIX = GLA_HEADS * GLA_DV + LRU_WIDTH
ODD_MIX = SWA_QH * HEAD_DIM + RET_HEADS * RET_DV

kernel_name = "hybrid_gla_rglru_swa_retention_dit"


def rms_norm(x, g):
    xf = x.astype(jnp.float32)
    y = xf * lax.rsqrt(jnp.mean(xf * xf, axis=-1, keepdims=True) + EPS)
    return (y * g.astype(jnp.float32)).astype(x.dtype)


def head_rms_norm(o, g):
    B, L = o.shape[:2]
    return rms_norm(o, g).reshape(B, L, -1)


def split_cols(z, sizes):
    return jnp.split(z, np.cumsum(sizes)[:-1].tolist(), axis=-1)


def axial_rope_tables(row, col):
    n_freq = HEAD_DIM // 4
    inv = ROPE_BASE ** (-jnp.arange(n_freq, dtype=jnp.float32) / n_freq)
    ang = jnp.concatenate([row.astype(jnp.float32)[:, None] * inv[None],
                           col.astype(jnp.float32)[:, None] * inv[None]], axis=-1)
    return jnp.cos(ang), jnp.sin(ang)


def apply_rope(x, cos, sin):
    half = x.shape[-1] // 2
    x1, x2 = x[..., :half], x[..., half:]
    c, s = cos[None, :, None, :], sin[None, :, None, :]
    return jnp.concatenate([x1 * c - x2 * s, x1 * s + x2 * c], axis=-1).astype(x.dtype)


def chunked_gla(q, k, v, log_a, s0):
    B, L, H, dk = q.shape
    dv = v.shape[-1]
    n = L // CHUNK

    def to_chunks(z):
        return z.reshape(B, n, CHUNK, H, z.shape[-1]).astype(jnp.float32)

    qc, kc, vc, gc = to_chunks(q), to_chunks(k), to_chunks(v), to_chunks(log_a)
    b = jnp.cumsum(gc, axis=2)
    b_last = b[:, :, -1:]
    q_in = qc * jnp.exp(b)
    k_in = kc * jnp.exp(-b)
    k_st = kc * jnp.exp(b_last - b)
    mask = jnp.tril(jnp.ones((CHUNK, CHUNK), dtype=bool))
    att = jnp.where(mask, jnp.einsum('bnthd,bnshd->bnhts', q_in, k_in), 0.0)
    o_intra = jnp.einsum('bnhts,bnshv->bnthv', att, vc)
    d_state = jnp.einsum('bnshd,bnshv->bnhdv', k_st, vc)
    decay = jnp.exp(b_last[:, :, 0])

    def step(S, inp):
        ds_n, dec_n = inp
        return dec_n[..., None] * S + ds_n, S

    s_final, s_prev = lax.scan(step, s0.astype(jnp.float32),
                               (jnp.moveaxis(d_state, 1, 0), jnp.moveaxis(decay, 1, 0)))
    s_prev = jnp.moveaxis(s_prev, 0, 1)
    o_inter = jnp.einsum('bnthd,bnhdv->bnthv', q_in, s_prev)
    o = (o_intra + o_inter).reshape(B, L, H, dv)
    return o.astype(v.dtype), s_final


def bidir_chunked(qc, kc, vc, gc_fw, gc_bw, ql, kl, vl, gl_fw, gl_bw):
    B, _, H, dk = qc.shape
    dv = vc.shape[-1]
    s0 = jnp.zeros((B, H, dk, dv), jnp.float32)
    flip = lambda z: jnp.flip(z, axis=1)
    oc_f, sc_f = chunked_gla(qc, kc, vc, gc_fw, s0)
    oc_b, sc_b = chunked_gla(flip(qc), flip(kc), flip(vc), flip(gc_bw), s0)
    ol_f, _ = chunked_gla(ql, kl, vl, gl_fw, sc_f)
    ol_b, _ = chunked_gla(flip(ql), flip(kl), flip(vl), flip(gl_bw), sc_b)
    return oc_f + flip(oc_b), ol_f + flip(ol_b)


def centred_depthwise_conv(x, w, b):
    K, C = w.shape
    left = K // 2
    y = lax.conv_general_dilated(x, w[:, None, :].astype(x.dtype), window_strides=(1,),
                                 padding=[(left, K - 1 - left)],
                                 dimension_numbers=('NWC', 'WIO', 'NWC'), feature_group_count=C)
    return y + b.astype(x.dtype)


def rg_lru_coeffs(xc, wa, ba, wx, bx, lam):
    B, L, W = xc.shape
    xb = xc.reshape(B, L, LRU_BLOCKS, W // LRU_BLOCKS)
    r = jax.nn.sigmoid(jnp.einsum('blni,nij->blnj', xb, wa).reshape(B, L, W) + ba)
    i_g = jax.nn.sigmoid(jnp.einsum('blni,nij->blnj', xb, wx).reshape(B, L, W) + bx)
    log_a = (LRU_C * r.astype(jnp.float32)) * jax.nn.log_sigmoid(lam.astype(jnp.float32))
    a = jnp.exp(log_a)
    u = jnp.sqrt(-jnp.expm1(2.0 * log_a)) * (i_g * xc).astype(jnp.float32)
    return a, u


def linear_recurrence(a, u, h0):
    def combine(l, r):
        al, ul = l
        ar, ur = r
        return al * ar, ar * ul + ur
    a_cum, h = lax.associative_scan(combine, (a, u), axis=1)
    h = h + a_cum * h0[:, None]
    return h, h[:, -1]


def bidir_lru(c_fw, c_bw, l_fw, l_bw):
    flip = lambda z: jnp.flip(z, axis=1)
    B, _, W = c_fw[0].shape
    h0 = jnp.zeros((B, W), jnp.float32)
    hc_f, sc_f = linear_recurrence(c_fw[0], c_fw[1], h0)
    hc_b, sc_b = linear_recurrence(flip(c_bw[0]), flip(c_bw[1]), h0)
    hl_f, _ = linear_recurrence(l_fw[0], l_fw[1], sc_f)
    hl_b, _ = linear_recurrence(flip(l_bw[0]), flip(l_bw[1]), sc_b)
    return hc_f + flip(hc_b), hl_f + flip(hl_b)


def gla_lru_mixer(hc, hl, w_in, gla_up_fw, gla_b_fw, gla_up_bw, gla_b_bw, gla_norm_g,
                  conv_w, conv_b, wa_fw, ba_fw, wx_fw, bx_fw, lam_fw,
                  wa_bw, ba_bw, wx_bw, bx_bw, lam_bw):
    def project(h):
        B, L, _ = h.shape
        q, k, v, g_gla, lr, xr, g_lru = split_cols(h @ w_in, EVEN_SIZES)
        q = q.reshape(B, L, GLA_HEADS, GLA_DK) * (GLA_DK ** -0.5)
        k = k.reshape(B, L, GLA_HEADS, GLA_DK)
        v = v.reshape(B, L, GLA_HEADS, GLA_DV)
        lr = lr.astype(jnp.float32)
        la_fw = (jax.nn.log_sigmoid(lr[..., :GLA_RANK] @ gla_up_fw + gla_b_fw) / GLA_TEMP
                 ).reshape(B, L, GLA_HEADS, GLA_DK)
        la_bw = (jax.nn.log_sigmoid(lr[..., GLA_RANK:] @ gla_up_bw + gla_b_bw) / GLA_TEMP
                 ).reshape(B, L, GLA_HEADS, GLA_DK)
        xc = centred_depthwise_conv(xr, conv_w, conv_b)
        co_fw = rg_lru_coeffs(xc, wa_fw, ba_fw, wx_fw, bx_fw, lam_fw)
        co_bw = rg_lru_coeffs(xc, wa_bw, ba_bw, wx_bw, bx_bw, lam_bw)
        return (q, k, v, la_fw, la_bw, g_gla), (co_fw, co_bw, g_lru)

    (qc, kc, vc, fc, bc, ggc), (lcf, lcb, glc) = project(hc)
    (ql, kl, vl, fl, bl, ggl), (llf, llb, gll) = project(hl)
    gla_c, gla_l = bidir_chunked(qc, kc, vc, fc, bc, ql, kl, vl, fl, bl)
    lru_c, lru_l = bidir_lru(lcf, lcb, llf, llb)
    y_c = jnp.concatenate([head_rms_norm(gla_c, gla_norm_g) * jax.nn.silu(ggc),
                           lru_c.astype(hc.dtype) * jax.nn.silu(glc)], axis=-1)
    y_l = jnp.concatenate([head_rms_norm(gla_l, gla_norm_g) * jax.nn.silu(ggl),
                           lru_l.astype(hl.dtype) * jax.nn.silu(gll)], axis=-1)
    return y_c, y_l


def context_attention(qc, kc, vc, sink):
    B, Lc, Hq, dh = qc.shape
    Hkv = kc.shape[2]
    G = Hq // Hkv
    qg = qc.reshape(B, Lc, Hkv, G, dh)
    s = jnp.einsum('bqhgd,bkhd->bhgqk', qg, kc).astype(jnp.float32) * (dh ** -0.5)
    sk = jnp.broadcast_to(sink.astype(jnp.float32).reshape(1, Hkv, G, 1, 1), (B, Hkv, G, Lc, 1))
    p = jax.nn.softmax(jnp.concatenate([s, sk], axis=-1), axis=-1)[..., :Lc].astype(vc.dtype)
    return jnp.einsum('bhgqk,bkhd->bqhgd', p, vc).reshape(B, Lc, Hq * dh)


def windowed_attention(q, k, v, kc, vc, sink):
    B, L, Hq, dh = q.shape
    Hkv = k.shape[2]
    G = Hq // Hkv
    Lc = kc.shape[1]
    nb = L // BLOCK
    pad = ((0, 0), (BLOCK, BLOCK), (0, 0), (0, 0))
    kp, vp = jnp.pad(k, pad), jnp.pad(v, pad)
    scale = dh ** -0.5
    sk = jnp.broadcast_to(sink.astype(jnp.float32).reshape(1, Hkv, G, 1, 1), (B, Hkv, G, BLOCK, 1))
    q_off = jnp.arange(BLOCK)[:, None]
    k_off = jnp.arange(3 * BLOCK)[None, :] - BLOCK
    near = jnp.abs(k_off - q_off) <= WINDOW

    def one_block(i):
        start = i * BLOCK
        qi = lax.dynamic_slice_in_dim(q, start, BLOCK, axis=1).reshape(B, BLOCK, Hkv, G, dh)
        ki = lax.dynamic_slice_in_dim(kp, start, 3 * BLOCK, axis=1)
        vi = lax.dynamic_slice_in_dim(vp, start, 3 * BLOCK, axis=1)
        kpos = start + k_off
        valid = near & (kpos >= 0) & (kpos < L)
        s_loc = jnp.einsum('bqhgd,bkhd->bhgqk', qi, ki).astype(jnp.float32) * scale
        s_loc = jnp.where(valid, s_loc, NEG_INF)
        s_ctx = jnp.einsum('bqhgd,bkhd->bhgqk', qi, kc).astype(jnp.float32) * scale
        p = jax.nn.softmax(jnp.concatenate([s_loc, s_ctx, sk], axis=-1), axis=-1).astype(v.dtype)
        o = (jnp.einsum('bhgqk,bkhd->bqhgd', p[..., :3 * BLOCK], vi)
             + jnp.einsum('bhgqk,bkhd->bqhgd', p[..., 3 * BLOCK:3 * BLOCK + Lc], vc))
        return o.reshape(B, BLOCK, Hq * dh)

    o = lax.map(one_block, jnp.arange(nb))
    return jnp.moveaxis(o, 0, 1).reshape(B, L, Hq * dh)


def swa_ret_mixer(hc, hl, cos, sin, w_in, sink, dec_fw, dec_bw, ret_norm_g):
    def project(h, rope):
        B, L, _ = h.shape
        q, k, v, g_swa, rq, rk, rv, g_ret = split_cols(h @ w_in, ODD_SIZES)
        q = q.reshape(B, L, SWA_QH, HEAD_DIM)
        k = k.reshape(B, L, SWA_KVH, HEAD_DIM)
        v = v.reshape(B, L, SWA_KVH, HEAD_DIM)
        rq = rq.reshape(B, L, RET_HEADS, RET_DK)
        rk = rk.reshape(B, L, RET_HEADS, RET_DK)
        rv = rv.reshape(B, L, RET_HEADS, RET_DV)
        if rope:
            q, k, rq, rk = (apply_rope(t, cos, sin) for t in (q, k, rq, rk))
        rk = rk * (RET_DK ** -0.5)
        return q, k, v, g_swa, rq, rk, rv, g_ret

    qc, kc, vc, gsc, rqc, rkc, rvc, grc = project(hc, False)
    ql, kl, vl, gsl, rql, rkl, rvl, grl = project(hl, True)
    att_c = context_attention(qc, kc, vc, sink)
    att_l = windowed_attention(ql, kl, vl, kc, vc, sink)

    def log_decay(logit, ref):
        return jnp.broadcast_to(jax.nn.log_sigmoid(logit.astype(jnp.float32))[:, None], ref.shape)

    ret_c, ret_l = bidir_chunked(rqc, rkc, rvc, log_decay(dec_fw, rkc), log_decay(dec_bw, rkc),
                                 rql, rkl, rvl, log_decay(dec_fw, rkl), log_decay(dec_bw, rkl))
    y_c = jnp.concatenate([att_c * jax.nn.silu(gsc),
                           head_rms_norm(ret_c, ret_norm_g) * jax.nn.silu(grc)], axis=-1)
    y_l = jnp.concatenate([att_l * jax.nn.silu(gsl),
                           head_rms_norm(ret_l, ret_norm_g) * jax.nn.silu(grl)], axis=-1)
    return y_c, y_l


def setup_inputs(seed: int = 0) -> dict:
    key = jax.random.key(seed)
    ks = iter(jax.random.split(key, 48))
    D = D_MODEL
    NE = (DEPTH + 1) // 2
    NO = DEPTH // 2
    lb = LRU_WIDTH // LRU_BLOCKS

    def nrm(shape, s):
        return jax.random.normal(next(ks), shape, jnp.float32) * s

    def lru_lambda():
        a0 = jax.random.uniform(next(ks), (NE, LRU_WIDTH), jnp.float32, 0.9, 0.999)
        u = a0 ** (1.0 / LRU_C)
        return jnp.log(u) - jnp.log1p(-u)

    m = 5.0 + jnp.arange(RET_HEADS, dtype=jnp.float32)
    ret_logit = jnp.log(2.0 ** m - 1.0)
    return {
        "x": nrm((BATCH, SEQ, D), 1.0),
        "c": nrm((BATCH, D), 1.0),
        "ctx": nrm((BATCH, CTX_LEN, D), 1.0),
        "c_ctx": nrm((D,), 1.0),
        "ada_w": nrm((DEPTH, D, 3 * D), 0.5 * D ** -0.5),
        "ada_b": nrm((DEPTH, 3 * D), 0.02),
        "norm_g": 1.0 + nrm((DEPTH, D), 0.02),
        "e_w_in": nrm((NE, D, EVEN_IN), D ** -0.5),
        "gla_up_fw": nrm((NE, GLA_RANK, GLA_HEADS * GLA_DK), GLA_RANK ** -0.5),
        "gla_b_fw": nrm((NE, GLA_HEADS * GLA_DK), 0.1),
        "gla_up_bw": nrm((NE, GLA_RANK, GLA_HEADS * GLA_DK), GLA_RANK ** -0.5),
        "gla_b_bw": nrm((NE, GLA_HEADS * GLA_DK), 0.1),
        "gla_norm_g": 1.0 + nrm((NE, GLA_DV), 0.02),
        "lru_conv_w": nrm((NE, LRU_CONV, LRU_WIDTH), LRU_CONV ** -0.5),
        "lru_conv_b": nrm((NE, LRU_WIDTH), 0.02),
        "lru_wa_fw": nrm((NE, LRU_BLOCKS, lb, lb), lb ** -0.5),
        "lru_ba_fw": nrm((NE, LRU_WIDTH), 0.02),
        "lru_wx_fw": nrm((NE, LRU_BLOCKS, lb, lb), lb ** -0.5),
        "lru_bx_fw": nrm((NE, LRU_WIDTH), 0.02),
        "lru_lam_fw": lru_lambda(),
        "lru_wa_bw": nrm((NE, LRU_BLOCKS, lb, lb), lb ** -0.5),
        "lru_ba_bw": nrm((NE, LRU_WIDTH), 0.02),
        "lru_wx_bw": nrm((NE, LRU_BLOCKS, lb, lb), lb ** -0.5),
        "lru_bx_bw": nrm((NE, LRU_WIDTH), 0.02),
        "lru_lam_bw": lru_lambda(),
        "e_w_out": nrm((NE, EVEN_MIX, D), EVEN_MIX ** -0.5),
        "o_w_in": nrm((NO, D, ODD_IN), D ** -0.5),
        "swa_sink": nrm((NO, SWA_QH), 0.5),
        "ret_dec_fw": ret_logit[None] + nrm((NO, RET_HEADS), 0.05),
        "ret_dec_bw": ret_logit[None] + nrm((NO, RET_HEADS), 0.05),
        "ret_norm_g": 1.0 + nrm((NO, RET_DV), 0.02),
        "o_w_out": nrm((NO, ODD_MIX, D), ODD_MIX ** -0.5),
        "final_g": 1.0 + nrm((D,), 0.02),
    }


def reference(x, c, ctx, c_ctx, ada_w, ada_b, norm_g, e_w_in, gla_up_fw, gla_b_fw, gla_up_bw, gla_b_bw,
              gla_norm_g, lru_conv_w, lru_conv_b, lru_wa_fw, lru_ba_fw, lru_wx_fw, lru_bx_fw, lru_lam_fw,
              lru_wa_bw, lru_ba_bw, lru_wx_bw, lru_bx_bw, lru_lam_bw, e_w_out, o_w_in, swa_sink,
              ret_dec_fw, ret_dec_bw, ret_norm_g, o_w_out, final_g):
    n_tok = x.shape[1]
    rows = n_tok // GRID_W
    row = jnp.repeat(jnp.arange(rows), GRID_W)
    col = jnp.tile(jnp.arange(GRID_W), rows)
    cos, sin = axial_rope_tables(row, col)
    D = x.shape[-1]
    ctx_h = ctx
    for i in range(DEPTH):
        mod_l = jax.nn.silu(c) @ ada_w[i] + ada_b[i]
        mod_c = jax.nn.silu(c_ctx) @ ada_w[i] + ada_b[i]
        sh_l, sc_l, g_l = mod_l[:, :D], mod_l[:, D:2 * D], mod_l[:, 2 * D:]
        sh_c, sc_c, g_c = mod_c[:D], mod_c[D:2 * D], mod_c[2 * D:]
        hl = rms_norm(x, norm_g[i]) * (1.0 + sc_l[:, None]) + sh_l[:, None]
        hc = rms_norm(ctx_h, norm_g[i]) * (1.0 + sc_c) + sh_c
        j = i // 2
        if i % 2 == 0:
            y_c, y_l = gla_lru_mixer(hc, hl, e_w_in[j], gla_up_fw[j], gla_b_fw[j], gla_up_bw[j], gla_b_bw[j],
                                     gla_norm_g[j], lru_conv_w[j], lru_conv_b[j],
                                     lru_wa_fw[j], lru_ba_fw[j], lru_wx_fw[j], lru_bx_fw[j], lru_lam_fw[j],
                                     lru_wa_bw[j], lru_ba_bw[j], lru_wx_bw[j], lru_bx_bw[j], lru_lam_bw[j])
            w_out = e_w_out[j]
        else:
            y_c, y_l = swa_ret_mixer(hc, hl, cos, sin, o_w_in[j], swa_sink[j], ret_dec_fw[j], ret_dec_bw[j],
                                     ret_norm_g[j])
            w_out = o_w_out[j]
        x = x + g_l[:, None] * (y_l @ w_out)
        if i < DEPTH - 1:
            ctx_h = ctx_h + g_c * (y_c @ w_out)
    return rms_norm(x, final_g)
```

```python
import functools

import numpy as np
import jax
import jax.numpy as jnp
from jax import lax
from jax.experimental import pallas as pl
from jax.experimental.pallas import tpu as pltpu

F32 = jnp.float32
BF16 = jnp.bfloat16

D_MODEL = 1024
CTX_LEN = 256
GRID_W = 64
EPS = 1e-6
NEG_INF = -1e30
ROPE_BASE = 10000.0
CHUNK = 64

GLA_HEADS = 4
GLA_DK = 64
GLA_DV = 128
GLA_RANK = 16
GLA_TEMP = 16.0
LRU_WIDTH = 512
LRU_BLOCKS = 4
LRU_CONV = 4
LRU_C = 8.0
HEAD_DIM = 64
SWA_QH = 8
SWA_KVH = 2
WINDOW = 128
BLOCK = 128
RET_HEADS = 4
RET_DK = 64
RET_DV = 128

LANES = 128
SUBLANES = 8
TM = 256
CTX_ROW = SUBLANES - 1
QK_W = GLA_HEADS * GLA_DK
V_W = GLA_HEADS * GLA_DV
VMEM_LIMIT = 48 * 1024 * 1024


def _params(sem):
    return pltpu.CompilerParams(dimension_semantics=sem, vmem_limit_bytes=VMEM_LIMIT)


def _dot(a, b):
    return jnp.dot(a, b, preferred_element_type=F32)


def _dot_nt(a, b):
    return lax.dot_general(a, b, (((1,), (1,)), ((), ())), preferred_element_type=F32)


def _dot_tn(a, b):
    return lax.dot_general(a, b, (((0,), (0,)), ((), ())), preferred_element_type=F32)


def _sigmoid(x):
    return 1.0 / (1.0 + jnp.exp(-x))


def _silu(x):
    return x * _sigmoid(x)


def _log_sigmoid(x):
    return jnp.minimum(x, 0.0) - jnp.log(1.0 + jnp.exp(-jnp.abs(x)))


def _ada_kernel(c_ref, w_ref, b_ref, o_ref):
    s = _silu(c_ref[...])
    o_ref[0] = _dot(s.astype(BF16), w_ref[0].astype(BF16)) + b_ref[0]


def _ada_mod(cvec, ada_w, ada_b):
    depth, d, d3 = ada_w.shape
    nsplit = d3 // d
    return pl.pallas_call(
        _ada_kernel,
        grid=(depth, nsplit),
        in_specs=[
            pl.BlockSpec((SUBLANES, d), lambda i, n: (0, 0)),
            pl.BlockSpec((1, d, d), lambda i, n: (i, 0, n)),
            pl.BlockSpec((1, 1, d), lambda i, n: (i, 0, n)),
        ],
        out_specs=pl.BlockSpec((1, SUBLANES, d), lambda i, n: (i, 0, n)),
        out_shape=jax.ShapeDtypeStruct((depth, SUBLANES, d3), F32),
        compiler_params=_params(("arbitrary", "arbitrary")),
        name="ada_mod",
    )(cvec, ada_w, ada_b.reshape(depth, 1, d3))


def _modulated(x_ref, mod_ref, g_ref):
    d = D_MODEL
    row = jnp.where(pl.program_id(1) == 0, CTX_ROW, pl.program_id(0))
    m = mod_ref[pl.ds(row, 1), :]
    x = x_ref[0]
    ms = jnp.mean(x * x, axis=-1, keepdims=True)
    return (x * lax.rsqrt(ms + EPS)) * g_ref[...] * (1.0 + m[:, d:2 * d]) + m[:, 0:d]


def _proj_even_kernel(x_ref, mod_ref, g_ref, w_ref, up_ref, ub_ref,
                      q_ref, k_ref, v_ref, gg_ref, laf_ref, lab_ref, xr_ref, gl_ref):
    h = _modulated(x_ref, mod_ref, g_ref)
    z = _dot(h.astype(BF16), w_ref[...])
    q_ref[0] = z[:, 0:256] * (GLA_DK ** -0.5)
    k_ref[0] = z[:, 256:512]
    v_ref[0] = z[:, 512:1024]
    gg_ref[0] = z[:, 1024:1536]
    xr_ref[0] = z[:, 1536:2048]
    gl_ref[0] = z[:, 2048:2560]
    pre = _dot(z[:, 2560:2688].astype(BF16), up_ref[...]) + ub_ref[...]
    la = _log_sigmoid(pre) * (1.0 / GLA_TEMP)
    laf_ref[0] = la[:, 0:QK_W]
    lab_ref[0] = la[:, QK_W:2 * QK_W]


def _proj_even(xs, mod, g, w_all, up_pad, ub):
    bsz, t, d = xs.shape
    nt = t // TM
    tok = lambda w: pl.BlockSpec((1, TM, w), lambda b, j: (b, j, 0))
    full = lambda a: pl.BlockSpec(a.shape, lambda b, j: (0,) * a.ndim)
    widths = (QK_W, QK_W, V_W, V_W, QK_W, QK_W, LRU_WIDTH, LRU_WIDTH)
    return pl.pallas_call(
        _proj_even_kernel,
        grid=(bsz, nt),
        in_specs=[tok(d), full(mod), full(g), full(w_all), full(up_pad), full(ub)],
        out_specs=[tok(w) for w in widths],
        out_shape=[jax.ShapeDtypeStruct((bsz, t, w), F32) for w in widths],
        compiler_params=_params(("arbitrary", "arbitrary")),
        name="proj_even",
    )(xs, mod, g, w_all, up_pad, ub)


def _rope(x, cos_f, sin_s, lane):
    outs = []
    for g in range(x.shape[1] // LANES):
        xg = x[:, g * LANES:(g + 1) * LANES]
        up = pltpu.roll(xg, LANES - HEAD_DIM // 2, axis=1)
        dn = pltpu.roll(xg, HEAD_DIM // 2, axis=1)
        swapped = jnp.where(lane % HEAD_DIM < HEAD_DIM // 2, up, dn)
        outs.append(xg * cos_f + swapped * sin_s)
    return outs[0] if len(outs) == 1 else jnp.concatenate(outs, axis=1)


def _proj_odd_kernel(x_ref, mod_ref, g_ref, w_ref, cos_ref, sin_ref,
                     q_ref, k_ref, v_ref, gs_ref, rq_ref, rk_ref, rv_ref, gr_ref):
    h = _modulated(x_ref, mod_ref, g_ref)
    z = _dot(h.astype(BF16), w_ref[...])
    cos_f = cos_ref[...]
    sin_s = sin_ref[...]
    lane = lax.broadcasted_iota(jnp.int32, (TM, LANES), 1)
    q_ref[0] = _rope(z[:, 0:512], cos_f, sin_s, lane)
    k_ref[0] = _rope(z[:, 512:640], cos_f, sin_s, lane)
    v_ref[0] = z[:, 640:768]
    gs_ref[0] = z[:, 768:1280]
    rq_ref[0] = _rope(z[:, 1280:1536], cos_f, sin_s, lane)
    rk_ref[0] = _rope(z[:, 1536:1792], cos_f, sin_s, lane) * (RET_DK ** -0.5)
    rv_ref[0] = z[:, 1792:2304]
    gr_ref[0] = z[:, 2304:2816]


def _proj_odd(xs, mod, g, w_in, cos_f, sin_s):
    bsz, t, d = xs.shape
    nt = t // TM
    tok = lambda w: pl.BlockSpec((1, TM, w), lambda b, j: (b, j, 0))
    full = lambda a: pl.BlockSpec(a.shape, lambda b, j: (0,) * a.ndim)
    tab = pl.BlockSpec((TM, LANES), lambda b, j: (j, 0))
    widths = (SWA_QH * HEAD_DIM, SWA_KVH * HEAD_DIM, SWA_KVH * HEAD_DIM, SWA_QH * HEAD_DIM,
              QK_W, QK_W, V_W, V_W)
    return pl.pallas_call(
        _proj_odd_kernel,
        grid=(bsz, nt),
        in_specs=[tok(d), full(mod), full(g), full(w_in), tab, tab],
        out_specs=[tok(w) for w in widths],
        out_shape=[jax.ShapeDtypeStruct((bsz, t, w), F32) for w in widths],
        compiler_params=_params(("arbitrary", "arbitrary")),
        name="proj_odd",
    )(xs, mod, g, w_in, cos_f, sin_s)


def _bwd_tile(j, nt):
    return jnp.where(j == 0, 0, nt - j)


def _scan_one_direction(q, k, v, bcum, s_ref, o_ref, reverse):
    nch = TM // CHUNK
    col_k = lax.broadcasted_iota(jnp.int32, (CHUNK, QK_W), 1)
    row_k = lax.broadcasted_iota(jnp.int32, (CHUNK, QK_W), 0)
    col_v = lax.broadcasted_iota(jnp.int32, (CHUNK, V_W), 1)
    if reverse:
        tri = (col_k % CHUNK) >= row_k
    else:
        tri = (col_k % CHUNK) <= row_k
    bd = (lax.broadcasted_iota(jnp.int32, (V_W, QK_W), 0) // GLA_DV
          == lax.broadcasted_iota(jnp.int32, (V_W, QK_W), 1) // GLA_DK)
    for c in (range(nch - 1, -1, -1) if reverse else range(nch)):
        r0 = c * CHUNK
        b = bcum[r0:r0 + CHUNK]
        last = b[0:1] if reverse else b[CHUNK - 1:CHUNK]
        qc, kc, vc = q[r0:r0 + CHUNK], k[r0:r0 + CHUNK], v[r0:r0 + CHUNK]
        q_in = (qc * jnp.exp(b)).astype(BF16)
        k_in = kc * jnp.exp(-b)
        k_st = (kc * jnp.exp(last - b)).astype(BF16)
        decay = jnp.exp(last)
        k_bd = jnp.concatenate(
            [jnp.where(col_k // GLA_DK == h, k_in, 0.0) for h in range(GLA_HEADS)], axis=0).astype(BF16)
        v_bd = jnp.concatenate(
            [jnp.where(col_v // GLA_DV == h, vc, 0.0) for h in range(GLA_HEADS)], axis=0).astype(BF16)
        att = jnp.where(tri, _dot_nt(q_in, k_bd), 0.0)
        s_t = s_ref[...]
        o_ref[0, r0:r0 + CHUNK, :] = _dot(att.astype(BF16), v_bd) + _dot_nt(q_in, s_t.astype(BF16))
        d_st = _dot_tn(vc.astype(BF16), k_st)
        s_ref[...] = s_t * decay + jnp.where(bd, d_st, 0.0)


def _cumsum_chunks(la, tri_ref):
    hi = la.astype(BF16)
    lo = (la - hi.astype(F32)).astype(BF16)
    return _dot(tri_ref[...], hi) + _dot(tri_ref[...], lo)


def _gla_kernel(qf, kf, vf, laf, qb, kb, vb, lab, trif, trib, of_ref, ob_ref, sf_ref, sb_ref):
    @pl.when(pl.program_id(1) == 0)
    def _():
        sf_ref[...] = jnp.zeros_like(sf_ref)
        sb_ref[...] = jnp.zeros_like(sb_ref)

    _scan_one_direction(qf[0], kf[0], vf[0], _cumsum_chunks(laf[0], trif), sf_ref, of_ref, False)
    _scan_one_direction(qb[0], kb[0], vb[0], _cumsum_chunks(lab[0], trib), sb_ref, ob_ref, True)


def _ret_kernel(qf, kf, vf, qb, kb, vb, ldf, ldb, of_ref, ob_ref, sf_ref, sb_ref):
    @pl.when(pl.program_id(1) == 0)
    def _():
        sf_ref[...] = jnp.zeros_like(sf_ref)
        sb_ref[...] = jnp.zeros_like(sb_ref)

    pos = lax.broadcasted_iota(jnp.int32, (TM, QK_W), 0) % CHUNK
    bf = (pos + 1).astype(F32) * _log_sigmoid(ldf[...])
    bb = (CHUNK - pos).astype(F32) * _log_sigmoid(ldb[...])
    _scan_one_direction(qf[0], kf[0], vf[0], bf, sf_ref, of_ref, False)
    _scan_one_direction(qb[0], kb[0], vb[0], bb, sb_ref, ob_ref, True)


def _chunk_tri(reverse):
    t = np.arange(TM)
    same = (t[:, None] // CHUNK) == (t[None, :] // CHUNK)
    order = (t[None, :] >= t[:, None]) if reverse else (t[None, :] <= t[:, None])
    return jnp.asarray(same & order, dtype=BF16)


def _bidir_scan(q, k, v, la_f=None, la_b=None, ld_f=None, ld_b=None):
    bsz, t, _ = q.shape
    nt = t // TM
    fwd = lambda w: pl.BlockSpec((1, TM, w), lambda b, j: (b, j, 0))
    bwd = lambda w: pl.BlockSpec((1, TM, w), lambda b, j: (b, _bwd_tile(j, nt), 0))
    full = lambda a: pl.BlockSpec(a.shape, lambda b, j: (0,) * a.ndim)
    common = dict(
        grid=(bsz, nt),
        out_specs=[fwd(V_W), bwd(V_W)],
        out_shape=[jax.ShapeDtypeStruct((bsz, t, V_W), F32)] * 2,
        scratch_shapes=[pltpu.VMEM((V_W, QK_W), F32)] * 2,
        compiler_params=_params(("arbitrary", "arbitrary")),
    )
    if la_f is not None:
        tri_f, tri_b = _chunk_tri(False), _chunk_tri(True)
        return pl.pallas_call(
            _gla_kernel,
            in_specs=[fwd(QK_W), fwd(QK_W), fwd(V_W), fwd(QK_W),
                      bwd(QK_W), bwd(QK_W), bwd(V_W), bwd(QK_W), full(tri_f), full(tri_b)],
            name="gla_scan", **common,
        )(q, k, v, la_f, q, k, v, la_b, tri_f, tri_b)
    return pl.pallas_call(
        _ret_kernel,
        in_specs=[fwd(QK_W), fwd(QK_W), fwd(V_W), bwd(QK_W), bwd(QK_W), bwd(V_W), full(ld_f), full(ld_b)],
        name="ret_scan", **common,
    )(q, k, v, q, k, v, ld_f, ld_b)


def _lru_conv(x_ref, prev_ref, next_ref, cw_ref, cb_ref, tile, nt):
    prev_ok = tile >= 2
    next_ok = jnp.logical_and(tile >= 1, tile <= nt - 2)
    prev = jnp.where(prev_ok, prev_ref[0], 0.0)
    nxt = jnp.where(next_ok, next_ref[0], 0.0)
    ext = jnp.concatenate([prev, x_ref[0], nxt], axis=0)
    n = ext.shape[0]
    lo, hi = SUBLANES, SUBLANES + TM
    cw = cw_ref[...]
    acc = ext[lo:hi] * cw[2:3]
    acc += pltpu.roll(ext, 2, axis=0)[lo:hi] * cw[0:1]
    acc += pltpu.roll(ext, 1, axis=0)[lo:hi] * cw[1:2]
    acc += pltpu.roll(ext, n - 1, axis=0)[lo:hi] * cw[3:4]
    return acc + cb_ref[...]


def _lru_one_direction(xc, w_ref, ba_ref, bx_ref, lam_ref, carry_ref, o_ref, reverse):
    half = LRU_WIDTH // 2
    xcb = xc.astype(BF16)
    pre0 = _dot(xcb[:, 0:half], w_ref[0])
    pre1 = _dot(xcb[:, half:], w_ref[1])
    pa = jnp.concatenate([pre0[:, 0:half], pre1[:, 0:half]], axis=1) + ba_ref[...]
    px = jnp.concatenate([pre0[:, half:], pre1[:, half:]], axis=1) + bx_ref[...]
    r = _sigmoid(pa)
    ig = _sigmoid(px)
    log_a = (LRU_C * r) * _log_sigmoid(lam_ref[...])
    a = jnp.exp(log_a)
    u = jnp.sqrt(1.0 - a * a) * (ig * xc)

    ng = TM // SUBLANES
    a3 = a.reshape(ng, SUBLANES, LRU_WIDTH)
    u3 = u.reshape(ng, SUBLANES, LRU_WIDTH)
    sub = lax.broadcasted_iota(jnp.int32, (ng, SUBLANES, LRU_WIDTH), 1)
    k = 1
    while k < SUBLANES:
        if reverse:
            a_s = pltpu.roll(a3, SUBLANES - k, axis=1)
            u_s = pltpu.roll(u3, SUBLANES - k, axis=1)
            ok = sub < SUBLANES - k
        else:
            a_s = pltpu.roll(a3, k, axis=1)
            u_s = pltpu.roll(u3, k, axis=1)
            ok = sub >= k
        u3 = a3 * jnp.where(ok, u_s, 0.0) + u3
        a3 = a3 * jnp.where(ok, a_s, 1.0)
        k *= 2

    edge = 0 if reverse else SUBLANES - 1
    carry = carry_ref[...]
    starts = [None] * ng
    for g in (range(ng - 1, -1, -1) if reverse else range(ng)):
        starts[g] = carry
        carry = a3[g, edge:edge + 1, :] * carry + u3[g, edge:edge + 1, :]
    carry_ref[...] = carry
    for g in range(ng):
        o_ref[0, g * SUBLANES:(g + 1) * SUBLANES, :] = a3[g] * starts[g] + u3[g]


def _lru_kernel(xf, xfp, xfn, xb, xbp, xbn, cw, cb,
                wf, baf, bxf, lamf, wb, bab, bxb, lamb,
                hf_ref, hb_ref, cf_ref, cbk_ref, *, nt):
    j = pl.program_id(1)

    @pl.when(j == 0)
    def _():
        cf_ref[...] = jnp.zeros_like(cf_ref)
        cbk_ref[...] = jnp.zeros_like(cbk_ref)

    xc_f = _lru_conv(xf, xfp, xfn, cw, cb, j, nt)
    _lru_one_direction(xc_f, wf, baf, bxf, lamf, cf_ref, hf_ref, False)
    xc_b = _lru_conv(xb, xbp, xbn, cw, cb, _bwd_tile(j, nt), nt)
    _lru_one_direction(xc_b, wb, bab, bxb, lamb, cbk_ref, hb_ref, True)


def _lru_scan(xr, cw, cb, wf, baf, bxf, lamf, wb, bab, bxb, lamb):
    bsz, t, w = xr.shape
    nt = t // TM
    per = TM // SUBLANES
    nrow = t // SUBLANES
    f_t = lambda j: j
    b_t = lambda j: _bwd_tile(j, nt)

    def specs(tile_of):
        return [
            pl.BlockSpec((1, TM, w), lambda b, j: (b, tile_of(j), 0)),
            pl.BlockSpec((1, SUBLANES, w), lambda b, j: (b, jnp.maximum(tile_of(j) * per - 1, 0), 0)),
            pl.BlockSpec((1, SUBLANES, w), lambda b, j: (b, jnp.minimum((tile_of(j) + 1) * per, nrow - 1), 0)),
        ]

    full = lambda a: pl.BlockSpec(a.shape, lambda b, j: (0,) * a.ndim)
    consts = (cw, cb, wf, baf, bxf, lamf, wb, bab, bxb, lamb)
    return pl.pallas_call(
        functools.partial(_lru_kernel, nt=nt),
        grid=(bsz, nt),
        in_specs=specs(f_t) + specs(b_t) + [full(a) for a in consts],
        out_specs=[pl.BlockSpec((1, TM, w), lambda b, j: (b, j, 0)),
                   pl.BlockSpec((1, TM, w), lambda b, j: (b, b_t(j), 0))],
        out_shape=[jax.ShapeDtypeStruct((bsz, t, w), F32)] * 2,
        scratch_shapes=[pltpu.VMEM((1, w), F32)] * 2,
        compiler_params=_params(("arbitrary", "arbitrary")),
        name="lru_scan",
    )(xr, xr, xr, xr, xr, xr, *consts)


def _attn_kernel(q_ref, kp_ref, kc_ref, kn_ref, kx_ref, vp_ref, vc_ref, vn_ref, vx_ref, sink_ref,
                 o_ref, *, nb):
    h = pl.program_id(1)
    u = pl.program_id(2)
    cblk = CTX_LEN // BLOCK
    lane = lax.broadcasted_iota(jnp.int32, (3 * BLOCK + CTX_LEN, LANES), 1)
    low = lane < HEAD_DIM

    def split_heads(x):
        xr = pltpu.roll(x, HEAD_DIM, axis=1)
        to_low = jnp.where(h == 0, x, xr)
        to_high = jnp.where(h == 0, xr, x)
        return (jnp.where(low, to_low, 0.0).astype(BF16), jnp.where(low, 0.0, to_high).astype(BF16))

    keys = jnp.concatenate([kp_ref[0], kc_ref[0], kn_ref[0], kx_ref[0]], axis=0)
    vals = jnp.concatenate([vp_ref[0], vc_ref[0], vn_ref[0], vx_ref[0]], axis=0)
    k_lo, k_hi = split_heads(keys)
    v_lo, v_hi = split_heads(vals)

    q = q_ref[0] * (HEAD_DIM ** -0.5)
    q2 = jnp.concatenate([q[:, 0:LANES], q[:, LANES:2 * LANES]], axis=0).astype(BF16)

    nq = 2 * BLOCK
    nk = 3 * BLOCK + CTX_LEN
    col = lax.broadcasted_iota(jnp.int32, (nq, nk), 1)
    row = lax.broadcasted_iota(jnp.int32, (nq, nk), 0) % BLOCK
    kblk = u - 1 + col // BLOCK
    local_ok = (jnp.abs(col - BLOCK - row) <= WINDOW) & (kblk >= cblk) & (kblk <= nb - 1) & (u >= cblk)
    valid = local_ok | (col >= 3 * BLOCK)

    def head_pair(k_sel, v_sel, sink):
        s = jnp.where(valid, _dot_nt(q2, k_sel), NEG_INF)
        m = jnp.maximum(jnp.max(s, axis=-1, keepdims=True), sink)
        p = jnp.exp(s - m)
        den = jnp.sum(p, axis=-1, keepdims=True) + jnp.exp(sink - m)
        return _dot(p.astype(BF16), v_sel) / den

    o2 = head_pair(k_lo, v_lo, sink_ref[0, 0][:, 0:1]) + head_pair(k_hi, v_hi, sink_ref[0, 1][:, 0:1])
    o_ref[0] = jnp.concatenate([o2[0:BLOCK], o2[BLOCK:2 * BLOCK]], axis=1)


def _attention(q, k, v, sink_tab):
    bsz, t, _ = q.shape
    nb = t // BLOCK
    kv_w = SWA_KVH * HEAD_DIM
    qw = (SWA_QH // SWA_KVH) * HEAD_DIM
    at = lambda off: pl.BlockSpec(
        (1, BLOCK, kv_w), lambda b, h, u: (b, jnp.clip(u + off, 0, nb - 1), 0))
    ctx = pl.BlockSpec((1, CTX_LEN, kv_w), lambda b, h, u: (b, 0, 0))
    return pl.pallas_call(
        functools.partial(_attn_kernel, nb=nb),
        grid=(bsz, SWA_KVH, nb),
        in_specs=[pl.BlockSpec((1, BLOCK, qw), lambda b, h, u: (b, u, h)),
                  at(-1), at(0), at(1), ctx, at(-1), at(0), at(1), ctx,
                  pl.BlockSpec((1, 2, 2 * BLOCK, LANES), lambda b, h, u: (h, 0, 0, 0))],
        out_specs=pl.BlockSpec((1, BLOCK, qw), lambda b, h, u: (b, u, h)),
        out_shape=jax.ShapeDtypeStruct((bsz, t, SWA_QH * HEAD_DIM), F32),
        compiler_params=_params(("arbitrary", "arbitrary", "arbitrary")),
        name="swa_attn",
    )(q, k, k, k, k, v, v, v, v, sink_tab)


def _head_norm(o, g_row):
    outs = []
    for hd in range(o.shape[1] // LANES):
        oh = o[:, hd * LANES:(hd + 1) * LANES]
        ms = jnp.mean(oh * oh, axis=-1, keepdims=True)
        outs.append(oh * lax.rsqrt(ms + EPS) * g_row)
    return jnp.concatenate(outs, axis=1)


def _finish(x_ref, mod_ref, y, w_ref, fg_ref, o_ref, final, row):
    d = D_MODEL
    gate = mod_ref[pl.ds(row, 1), :][:, 2 * d:3 * d]
    xn = x_ref[0] + gate * _dot(y.astype(BF16), w_ref[...])
    if final:
        ms = jnp.mean(xn * xn, axis=-1, keepdims=True)
        xn = xn * lax.rsqrt(ms + EPS) * fg_ref[...]
    o_ref[0] = xn


def _out_even_kernel(x_ref, mod_ref, of_ref, ob_ref, gg_ref, hf_ref, hb_ref, gl_ref, ng_ref, w_ref, fg_ref,
                     o_ref, *, final):
    row = pl.program_id(0) if final else jnp.where(pl.program_id(1) == 0, CTX_ROW, pl.program_id(0))
    gla = _head_norm(of_ref[0] + ob_ref[0], ng_ref[...]) * _silu(gg_ref[0])
    lru = (hf_ref[0] + hb_ref[0]) * _silu(gl_ref[0])
    _finish(x_ref, mod_ref, jnp.concatenate([gla, lru], axis=1), w_ref, fg_ref, o_ref, final, row)


def _out_odd_kernel(x_ref, mod_ref, att_ref, gs_ref, of_ref, ob_ref, gr_ref, ng_ref, w_ref, fg_ref,
                    o_ref, *, final):
    row = pl.program_id(0) if final else jnp.where(pl.program_id(1) == 0, CTX_ROW, pl.program_id(0))
    att = att_ref[0] * _silu(gs_ref[0])
    ret = _head_norm(of_ref[0] + ob_ref[0], ng_ref[...]) * _silu(gr_ref[0])
    _finish(x_ref, mod_ref, jnp.concatenate([att, ret], axis=1), w_ref, fg_ref, o_ref, final, row)


def _out_proj(body, xs, mod, tok_inputs, ng, w_out, fg, final):
    bsz, t, d = xs.shape
    nt = t // TM
    skip = 1 if final else 0
    tok = lambda w: pl.BlockSpec((1, TM, w), lambda b, j: (b, j + skip, 0))
    full = lambda a: pl.BlockSpec(a.shape, lambda b, j: (0,) * a.ndim)
    return pl.pallas_call(
        functools.partial(body, final=final),
        grid=(bsz, nt - skip),
        in_specs=[tok(d), full(mod)] + [tok(a.shape[-1]) for a in tok_inputs]
                 + [full(ng), full(w_out), full(fg)],
        out_specs=pl.BlockSpec((1, TM, d), lambda b, j: (b, j, 0)),
        out_shape=jax.ShapeDtypeStruct((bsz, t - skip * TM, d), F32),
        compiler_params=_params(("arbitrary", "arbitrary")),
        name="out_proj",
    )(xs, mod, *tok_inputs, ng, w_out, fg)


def _block_diag_gates(wa, wx):
    lb = wa.shape[-1]
    z = jnp.zeros((lb, lb), wa.dtype)

    def pair(w, i):
        return jnp.concatenate([jnp.concatenate([w[2 * i], z], axis=1),
                                jnp.concatenate([z, w[2 * i + 1]], axis=1)], axis=0)

    return jnp.stack([jnp.concatenate([pair(wa, i), pair(wx, i)], axis=1) for i in range(2)]).astype(BF16)


def _rope_tables(n_tok):
    rows = n_tok // GRID_W
    row = jnp.repeat(jnp.arange(rows), GRID_W)
    col = jnp.tile(jnp.arange(GRID_W), rows)
    n_freq = HEAD_DIM // 4
    inv = ROPE_BASE ** (-jnp.arange(n_freq, dtype=F32) / n_freq)
    ang = jnp.concatenate([row.astype(F32)[:, None] * inv[None],
                           col.astype(F32)[:, None] * inv[None]], axis=-1)
    cos, sin = jnp.cos(ang), jnp.sin(ang)
    reps = LANES // HEAD_DIM
    cos_f = jnp.tile(cos, (1, 2 * reps))
    sin_s = jnp.tile(jnp.concatenate([-sin, sin], axis=-1), (1, reps))
    cos_f = jnp.concatenate([jnp.ones((CTX_LEN, LANES), F32), cos_f], axis=0)
    sin_s = jnp.concatenate([jnp.zeros((CTX_LEN, LANES), F32), sin_s], axis=0)
    return cos_f, sin_s


def kernel(x, c, ctx, c_ctx, ada_w, ada_b, norm_g, e_w_in, gla_up_fw, gla_b_fw, gla_up_bw, gla_b_bw, gla_norm_g, lru_conv_w, lru_conv_b, lru_wa_fw, lru_ba_fw, lru_wx_fw, lru_bx_fw, lru_lam_fw, lru_wa_bw, lru_ba_bw, lru_wx_bw, lru_bx_bw, lru_lam_bw, e_w_out, o_w_in, swa_sink, ret_dec_fw, ret_dec_bw, ret_norm_g, o_w_out, final_g):
    bsz, n_tok, d = x.shape
    depth = ada_w.shape[0]
    assert d == D_MODEL and ctx.shape[1] == CTX_LEN and n_tok % TM == 0 and bsz <= CTX_ROW

    cvec = jnp.zeros((SUBLANES, d), F32).at[0:bsz].set(c).at[CTX_ROW].set(c_ctx)
    mods = _ada_mod(cvec, ada_w, ada_b)
    xs = jnp.concatenate([ctx, x], axis=1)
    cos_f, sin_s = _rope_tables(n_tok)
    row = lambda a: a.reshape(1, -1)
    fg = row(final_g)

    for i in range(depth):
        jj = i // 2
        final = i == depth - 1
        mod = mods[i]
        g = row(norm_g[i])
        if i % 2 == 0:
            w = e_w_in[jj]
            w_all = jnp.concatenate(
                [w[:, 0:1536], w[:, 1568:2592], w[:, 1536:1568], jnp.zeros((d, LANES - 2 * GLA_RANK), F32)],
                axis=1).astype(BF16)
            up_pad = jnp.zeros((LANES, 2 * QK_W), F32)
            up_pad = up_pad.at[0:GLA_RANK, 0:QK_W].set(gla_up_fw[jj])
            up_pad = up_pad.at[GLA_RANK:2 * GLA_RANK, QK_W:].set(gla_up_bw[jj]).astype(BF16)
            ub = jnp.concatenate([gla_b_fw[jj], gla_b_bw[jj]]).reshape(1, -1)
            q, k, v, gg, la_f, la_b, xr, gl = _proj_even(xs, mod, g, w_all, up_pad, ub)
            o_f, o_b = _bidir_scan(q, k, v, la_f=la_f, la_b=la_b)
            h_f, h_b = _lru_scan(
                xr, lru_conv_w[jj], row(lru_conv_b[jj]),
                _block_diag_gates(lru_wa_fw[jj], lru_wx_fw[jj]), row(lru_ba_fw[jj]), row(lru_bx_fw[jj]),
                row(lru_lam_fw[jj]),
                _block_diag_gates(lru_wa_bw[jj], lru_wx_bw[jj]), row(lru_ba_bw[jj]), row(lru_bx_bw[jj]),
                row(lru_lam_bw[jj]))
            xs = _out_proj(_out_even_kernel, xs, mod, (o_f, o_b, gg, h_f, h_b, gl),
                           row(gla_norm_g[jj]), e_w_out[jj].astype(BF16), fg, final)
        else:
            q, k, v, gs, rq, rk, rv, gr = _proj_odd(xs, mod, g, o_w_in[jj].astype(BF16), cos_f, sin_s)
            sink = swa_sink[jj].reshape(SWA_KVH, 2, 2)
            sink_tab = jnp.broadcast_to(
                jnp.transpose(sink, (0, 2, 1))[:, :, :, None, None],
                (SWA_KVH, 2, 2, BLOCK, LANES)).reshape(SWA_KVH, 2, 2 * BLOCK, LANES)
            att = _attention(q, k, v, sink_tab)
            ld_f = row(jnp.repeat(ret_dec_fw[jj], RET_DK))
            ld_b = row(jnp.repeat(ret_dec_bw[jj], RET_DK))
            o_f, o_b = _bidir_scan(rq, rk, rv, ld_f=ld_f, ld_b=ld_b)
            xs = _out_proj(_out_odd_kernel, xs, mod, (att, gs, o_f, o_b, gr),
                           row(ret_norm_g[jj]), o_w_out[jj].astype(BF16), fg, final)
    return xs
```

```python
import functools

import numpy as np
import jax
import jax.numpy as jnp
from jax import lax
from jax.experimental import pallas as pl
from jax.experimental.pallas import tpu as pltpu

F32 = jnp.float32
BF16 = jnp.bfloat16

D_MODEL = 1024
CTX_LEN = 256
GRID_W = 64
EPS = 1e-6
NEG_INF = -1e30
ROPE_BASE = 10000.0
CHUNK = 64

GLA_HEADS = 4
GLA_DK = 64
GLA_DV = 128
GLA_RANK = 16
GLA_TEMP = 16.0
LRU_WIDTH = 512
LRU_BLOCKS = 4
LRU_CONV = 4
LRU_C = 8.0
HEAD_DIM = 64
SWA_QH = 8
SWA_KVH = 2
WINDOW = 128
BLOCK = 128
RET_HEADS = 4
RET_DK = 64
RET_DV = 128

LANES = 128
SUBLANES = 8
TM = 256
CTX_ROW = SUBLANES - 1
QK_W = GLA_HEADS * GLA_DK
V_W = GLA_HEADS * GLA_DV
VMEM_LIMIT = 48 * 1024 * 1024


def _params(sem):
    return pltpu.CompilerParams(dimension_semantics=sem, vmem_limit_bytes=VMEM_LIMIT)


def _dot(a, b):
    return jnp.dot(a, b, preferred_element_type=F32)


def _dot_nt(a, b):
    return lax.dot_general(a, b, (((1,), (1,)), ((), ())), preferred_element_type=F32)


def _dot_tn(a, b):
    return lax.dot_general(a, b, (((0,), (0,)), ((), ())), preferred_element_type=F32)


def _sigmoid(x):
    return 1.0 / (1.0 + jnp.exp(-x))


def _silu(x):
    return x * _sigmoid(x)


def _log_sigmoid(x):
    return jnp.minimum(x, 0.0) - jnp.log(1.0 + jnp.exp(-jnp.abs(x)))


def _ada_kernel(c_ref, w_ref, b_ref, o_ref):
    s = _silu(c_ref[...])
    o_ref[0] = _dot(s.astype(BF16), w_ref[0].astype(BF16)) + b_ref[0]


def _ada_mod(cvec, ada_w, ada_b):
    depth, d, d3 = ada_w.shape
    nsplit = d3 // d
    return pl.pallas_call(
        _ada_kernel,
        grid=(depth, nsplit),
        in_specs=[
            pl.BlockSpec((SUBLANES, d), lambda i, n: (0, 0)),
            pl.BlockSpec((1, d, d), lambda i, n: (i, 0, n)),
            pl.BlockSpec((1, 1, d), lambda i, n: (i, 0, n)),
        ],
        out_specs=pl.BlockSpec((1, SUBLANES, d), lambda i, n: (i, 0, n)),
        out_shape=jax.ShapeDtypeStruct((depth, SUBLANES, d3), F32),
        compiler_params=_params(("arbitrary", "arbitrary")),
        name="ada_mod",
    )(cvec, ada_w, ada_b.reshape(depth, 1, d3))


def _modulated(x_ref, mod_ref, g_ref):
    d = D_MODEL
    row = jnp.where(pl.program_id(1) == 0, CTX_ROW, pl.program_id(0))
    m = mod_ref[pl.ds(row, 1), :]
    x = x_ref[0]
    ms = jnp.mean(x * x, axis=-1, keepdims=True)
    return (x * lax.rsqrt(ms + EPS)) * g_ref[...] * (1.0 + m[:, d:2 * d]) + m[:, 0:d]


def _proj_even_kernel(x_ref, mod_ref, g_ref, w_ref, up_ref, ub_ref,
                      q_ref, k_ref, v_ref, gg_ref, laf_ref, lab_ref, xr_ref, gl_ref):
    h = _modulated(x_ref, mod_ref, g_ref)
    z = _dot(h.astype(BF16), w_ref[...])
    q_ref[0] = z[:, 0:256] * (GLA_DK ** -0.5)
    k_ref[0] = z[:, 256:512]
    v_ref[0] = z[:, 512:1024]
    gg_ref[0] = z[:, 1024:1536]
    xr_ref[0] = z[:, 1536:2048]
    gl_ref[0] = z[:, 2048:2560]
    pre = _dot(z[:, 2560:2688].astype(BF16), up_ref[...]) + ub_ref[...]
    la = _log_sigmoid(pre) * (1.0 / GLA_TEMP)
    laf_ref[0] = la[:, 0:QK_W]
    lab_ref[0] = la[:, QK_W:2 * QK_W]


def _proj_even(xs, mod, g, w_all, up_pad, ub):
    bsz, t, d = xs.shape
    nt = t // TM
    tok = lambda w: pl.BlockSpec((1, TM, w), lambda b, j: (b, j, 0))
    full = lambda a: pl.BlockSpec(a.shape, lambda b, j: (0,) * a.ndim)
    widths = (QK_W, QK_W, V_W, V_W, QK_W, QK_W, LRU_WIDTH, LRU_WIDTH)
    return pl.pallas_call(
        _proj_even_kernel,
        grid=(bsz, nt),
        in_specs=[tok(d), full(mod), full(g), full(w_all), full(up_pad), full(ub)],
        out_specs=[tok(w) for w in widths],
        out_shape=[jax.ShapeDtypeStruct((bsz, t, w), F32) for w in widths],
        compiler_params=_params(("arbitrary", "arbitrary")),
        name="proj_even",
    )(xs, mod, g, w_all, up_pad, ub)


def _rope(x, cos_f, sin_s, lane):
    outs = []
    for g in range(x.shape[1] // LANES):
        xg = x[:, g * LANES:(g + 1) * LANES]
        up = pltpu.roll(xg, LANES - HEAD_DIM // 2, axis=1)
        dn = pltpu.roll(xg, HEAD_DIM // 2, axis=1)
        swapped = jnp.where(lane % HEAD_DIM < HEAD_DIM // 2, up, dn)
        outs.append(xg * cos_f + swapped * sin_s)
    return outs[0] if len(outs) == 1 else jnp.concatenate(outs, axis=1)


def _proj_odd_kernel(x_ref, mod_ref, g_ref, w_ref, cos_ref, sin_ref,
                     q_ref, kk_ref, kr_ref, vv_ref, vr_ref, gs_ref, rq_ref, rk_ref, rv_ref, gr_ref):
    h = _modulated(x_ref, mod_ref, g_ref)
    z = _dot(h.astype(BF16), w_ref[...])
    cos_f = cos_ref[...]
    sin_s = sin_ref[...]
    lane = lax.broadcasted_iota(jnp.int32, (TM, LANES), 1)
    q_ref[0] = (_rope(z[:, 0:512], cos_f, sin_s, lane) * (HEAD_DIM ** -0.5)).astype(BF16)
    k = _rope(z[:, 512:640], cos_f, sin_s, lane)
    v = z[:, 640:768]
    kk_ref[0] = k.astype(BF16)
    kr_ref[0] = pltpu.roll(k, HEAD_DIM, axis=1).astype(BF16)
    vv_ref[0] = v.astype(BF16)
    vr_ref[0] = pltpu.roll(v, HEAD_DIM, axis=1).astype(BF16)
    gs_ref[0] = z[:, 768:1280]
    rq_ref[0] = _rope(z[:, 1280:1536], cos_f, sin_s, lane)
    rk_ref[0] = _rope(z[:, 1536:1792], cos_f, sin_s, lane) * (RET_DK ** -0.5)
    rv_ref[0] = z[:, 1792:2304]
    gr_ref[0] = z[:, 2304:2816]


def _proj_odd(xs, mod, g, w_in, cos_f, sin_s):
    bsz, t, d = xs.shape
    nt = t // TM
    tok = lambda w: pl.BlockSpec((1, TM, w), lambda b, j: (b, j, 0))
    full = lambda a: pl.BlockSpec(a.shape, lambda b, j: (0,) * a.ndim)
    tab = pl.BlockSpec((TM, LANES), lambda b, j: (j, 0))
    kv_w = SWA_KVH * HEAD_DIM
    outs = ((SWA_QH * HEAD_DIM, BF16), (kv_w, BF16), (kv_w, BF16), (kv_w, BF16), (kv_w, BF16),
            (SWA_QH * HEAD_DIM, F32), (QK_W, F32), (QK_W, F32), (V_W, F32), (V_W, F32))
    return pl.pallas_call(
        _proj_odd_kernel,
        grid=(bsz, nt),
        in_specs=[tok(d), full(mod), full(g), full(w_in), tab, tab],
        out_specs=[tok(w) for w, _ in outs],
        out_shape=[jax.ShapeDtypeStruct((bsz, t, w), dt) for w, dt in outs],
        compiler_params=_params(("arbitrary", "arbitrary")),
        name="proj_odd",
    )(xs, mod, g, w_in, cos_f, sin_s)


def _bwd_tile(j, nt):
    return jnp.where(j == 0, 0, nt - j)


def _scan_one_direction(q, k, v, bcum, s_ref, o_ref, reverse):
    nch = TM // CHUNK
    col_k = lax.broadcasted_iota(jnp.int32, (CHUNK, QK_W), 1)
    row_k = lax.broadcasted_iota(jnp.int32, (CHUNK, QK_W), 0)
    col_v = lax.broadcasted_iota(jnp.int32, (CHUNK, V_W), 1)
    if reverse:
        tri = (col_k % CHUNK) >= row_k
    else:
        tri = (col_k % CHUNK) <= row_k
    bd = (lax.broadcasted_iota(jnp.int32, (V_W, QK_W), 0) // GLA_DV
          == lax.broadcasted_iota(jnp.int32, (V_W, QK_W), 1) // GLA_DK)
    for c in (range(nch - 1, -1, -1) if reverse else range(nch)):
        r0 = c * CHUNK
        b = bcum[r0:r0 + CHUNK]
        last = b[0:1] if reverse else b[CHUNK - 1:CHUNK]
        qc, kc, vc = q[r0:r0 + CHUNK], k[r0:r0 + CHUNK], v[r0:r0 + CHUNK]
        q_in = (qc * jnp.exp(b)).astype(BF16)
        k_in = kc * jnp.exp(-b)
        k_st = (kc * jnp.exp(last - b)).astype(BF16)
        decay = jnp.exp(last)
        k_bd = jnp.concatenate(
            [jnp.where(col_k // GLA_DK == h, k_in, 0.0) for h in range(GLA_HEADS)], axis=0).astype(BF16)
        v_bd = jnp.concatenate(
            [jnp.where(col_v // GLA_DV == h, vc, 0.0) for h in range(GLA_HEADS)], axis=0).astype(BF16)
        att = jnp.where(tri, _dot_nt(q_in, k_bd), 0.0)
        s_t = s_ref[...]
        o_ref[0, r0:r0 + CHUNK, :] = _dot(att.astype(BF16), v_bd) + _dot_nt(q_in, s_t.astype(BF16))
        d_st = _dot_tn(vc.astype(BF16), k_st)
        s_ref[...] = s_t * decay + jnp.where(bd, d_st, 0.0)


def _cumsum_chunks(la, tri_ref):
    hi = la.astype(BF16)
    lo = (la - hi.astype(F32)).astype(BF16)
    return _dot(tri_ref[...], hi) + _dot(tri_ref[...], lo)


def _gla_kernel(qf, kf, vf, laf, qb, kb, vb, lab, trif, trib, of_ref, ob_ref, sf_ref, sb_ref):
    @pl.when(pl.program_id(1) == 0)
    def _():
        sf_ref[...] = jnp.zeros_like(sf_ref)
        sb_ref[...] = jnp.zeros_like(sb_ref)

    _scan_one_direction(qf[0], kf[0], vf[0], _cumsum_chunks(laf[0], trif), sf_ref, of_ref, False)
    _scan_one_direction(qb[0], kb[0], vb[0], _cumsum_chunks(lab[0], trib), sb_ref, ob_ref, True)


def _ret_kernel(qf, kf, vf, qb, kb, vb, ldf, ldb, of_ref, ob_ref, sf_ref, sb_ref):
    @pl.when(pl.program_id(1) == 0)
    def _():
        sf_ref[...] = jnp.zeros_like(sf_ref)
        sb_ref[...] = jnp.zeros_like(sb_ref)

    pos = lax.broadcasted_iota(jnp.int32, (TM, QK_W), 0) % CHUNK
    bf = (pos + 1).astype(F32) * _log_sigmoid(ldf[...])
    bb = (CHUNK - pos).astype(F32) * _log_sigmoid(ldb[...])
    _scan_one_direction(qf[0], kf[0], vf[0], bf, sf_ref, of_ref, False)
    _scan_one_direction(qb[0], kb[0], vb[0], bb, sb_ref, ob_ref, True)


def _chunk_tri(reverse):
    t = np.arange(TM)
    same = (t[:, None] // CHUNK) == (t[None, :] // CHUNK)
    order = (t[None, :] >= t[:, None]) if reverse else (t[None, :] <= t[:, None])
    return jnp.asarray(same & order, dtype=BF16)


def _bidir_scan(q, k, v, la_f=None, la_b=None, ld_f=None, ld_b=None):
    bsz, t, _ = q.shape
    nt = t // TM
    fwd = lambda w: pl.BlockSpec((1, TM, w), lambda b, j: (b, j, 0))
    bwd = lambda w: pl.BlockSpec((1, TM, w), lambda b, j: (b, _bwd_tile(j, nt), 0))
    full = lambda a: pl.BlockSpec(a.shape, lambda b, j: (0,) * a.ndim)
    common = dict(
        grid=(bsz, nt),
        out_specs=[fwd(V_W), bwd(V_W)],
        out_shape=[jax.ShapeDtypeStruct((bsz, t, V_W), F32)] * 2,
        scratch_shapes=[pltpu.VMEM((V_W, QK_W), F32)] * 2,
        compiler_params=_params(("arbitrary", "arbitrary")),
    )
    if la_f is not None:
        tri_f, tri_b = _chunk_tri(False), _chunk_tri(True)
        return pl.pallas_call(
            _gla_kernel,
            in_specs=[fwd(QK_W), fwd(QK_W), fwd(V_W), fwd(QK_W),
                      bwd(QK_W), bwd(QK_W), bwd(V_W), bwd(QK_W), full(tri_f), full(tri_b)],
            name="gla_scan", **common,
        )(q, k, v, la_f, q, k, v, la_b, tri_f, tri_b)
    return pl.pallas_call(
        _ret_kernel,
        in_specs=[fwd(QK_W), fwd(QK_W), fwd(V_W), bwd(QK_W), bwd(QK_W), bwd(V_W), full(ld_f), full(ld_b)],
        name="ret_scan", **common,
    )(q, k, v, q, k, v, ld_f, ld_b)


def _lru_conv(x_ref, prev_ref, next_ref, cw_ref, cb_ref, tile, nt):
    prev_ok = tile >= 2
    next_ok = jnp.logical_and(tile >= 1, tile <= nt - 2)
    prev = jnp.where(prev_ok, prev_ref[0], 0.0)
    nxt = jnp.where(next_ok, next_ref[0], 0.0)
    ext = jnp.concatenate([prev, x_ref[0], nxt], axis=0)
    n = ext.shape[0]
    lo, hi = SUBLANES, SUBLANES + TM
    cw = cw_ref[...]
    acc = ext[lo:hi] * cw[2:3]
    acc += pltpu.roll(ext, 2, axis=0)[lo:hi] * cw[0:1]
    acc += pltpu.roll(ext, 1, axis=0)[lo:hi] * cw[1:2]
    acc += pltpu.roll(ext, n - 1, axis=0)[lo:hi] * cw[3:4]
    return acc + cb_ref[...]


def _lru_one_direction(xc, w_ref, ba_ref, bx_ref, lam_ref, carry_ref, o_ref, reverse):
    half = LRU_WIDTH // 2
    xcb = xc.astype(BF16)
    pre0 = _dot(xcb[:, 0:half], w_ref[0])
    pre1 = _dot(xcb[:, half:], w_ref[1])
    pa = jnp.concatenate([pre0[:, 0:half], pre1[:, 0:half]], axis=1) + ba_ref[...]
    px = jnp.concatenate([pre0[:, half:], pre1[:, half:]], axis=1) + bx_ref[...]
    r = _sigmoid(pa)
    ig = _sigmoid(px)
    log_a = (LRU_C * r) * _log_sigmoid(lam_ref[...])
    a = jnp.exp(log_a)
    u = jnp.sqrt(1.0 - a * a) * (ig * xc)

    ng = TM // SUBLANES
    a3 = a.reshape(ng, SUBLANES, LRU_WIDTH)
    u3 = u.reshape(ng, SUBLANES, LRU_WIDTH)
    sub = lax.broadcasted_iota(jnp.int32, (ng, SUBLANES, LRU_WIDTH), 1)
    k = 1
    while k < SUBLANES:
        if reverse:
            a_s = pltpu.roll(a3, SUBLANES - k, axis=1)
            u_s = pltpu.roll(u3, SUBLANES - k, axis=1)
            ok = sub < SUBLANES - k
        else:
            a_s = pltpu.roll(a3, k, axis=1)
            u_s = pltpu.roll(u3, k, axis=1)
            ok = sub >= k
        u3 = a3 * jnp.where(ok, u_s, 0.0) + u3
        a3 = a3 * jnp.where(ok, a_s, 1.0)
        k *= 2

    edge = 0 if reverse else SUBLANES - 1
    carry = carry_ref[...]
    starts = [None] * ng
    for g in (range(ng - 1, -1, -1) if reverse else range(ng)):
        starts[g] = carry
        carry = a3[g, edge:edge + 1, :] * carry + u3[g, edge:edge + 1, :]
    carry_ref[...] = carry
    for g in range(ng):
        o_ref[0, g * SUBLANES:(g + 1) * SUBLANES, :] = a3[g] * starts[g] + u3[g]


def _lru_kernel(xf, xfp, xfn, xb, xbp, xbn, cw, cb,
                wf, baf, bxf, lamf, wb, bab, bxb, lamb,
                hf_ref, hb_ref, cf_ref, cbk_ref, *, nt):
    j = pl.program_id(1)

    @pl.when(j == 0)
    def _():
        cf_ref[...] = jnp.zeros_like(cf_ref)
        cbk_ref[...] = jnp.zeros_like(cbk_ref)

    xc_f = _lru_conv(xf, xfp, xfn, cw, cb, j, nt)
    _lru_one_direction(xc_f, wf, baf, bxf, lamf, cf_ref, hf_ref, False)
    xc_b = _lru_conv(xb, xbp, xbn, cw, cb, _bwd_tile(j, nt), nt)
    _lru_one_direction(xc_b, wb, bab, bxb, lamb, cbk_ref, hb_ref, True)


def _lru_scan(xr, cw, cb, wf, baf, bxf, lamf, wb, bab, bxb, lamb):
    bsz, t, w = xr.shape
    nt = t // TM
    per = TM // SUBLANES
    nrow = t // SUBLANES
    f_t = lambda j: j
    b_t = lambda j: _bwd_tile(j, nt)

    def specs(tile_of):
        return [
            pl.BlockSpec((1, TM, w), lambda b, j: (b, tile_of(j), 0)),
            pl.BlockSpec((1, SUBLANES, w), lambda b, j: (b, jnp.maximum(tile_of(j) * per - 1, 0), 0)),
            pl.BlockSpec((1, SUBLANES, w), lambda b, j: (b, jnp.minimum((tile_of(j) + 1) * per, nrow - 1), 0)),
        ]

    full = lambda a: pl.BlockSpec(a.shape, lambda b, j: (0,) * a.ndim)
    consts = (cw, cb, wf, baf, bxf, lamf, wb, bab, bxb, lamb)
    return pl.pallas_call(
        functools.partial(_lru_kernel, nt=nt),
        grid=(bsz, nt),
        in_specs=specs(f_t) + specs(b_t) + [full(a) for a in consts],
        out_specs=[pl.BlockSpec((1, TM, w), lambda b, j: (b, j, 0)),
                   pl.BlockSpec((1, TM, w), lambda b, j: (b, b_t(j), 0))],
        out_shape=[jax.ShapeDtypeStruct((bsz, t, w), F32)] * 2,
        scratch_shapes=[pltpu.VMEM((1, w), F32)] * 2,
        compiler_params=_params(("arbitrary", "arbitrary")),
        name="lru_scan",
    )(xr, xr, xr, xr, xr, xr, *consts)


N_KEYS = 3 * BLOCK + CTX_LEN


def _attn_kernel(q_ref, bias_ref, sink_ref, *refs):
    o_ref = refs[-1]
    kk, kr, vv, vr = (jnp.concatenate([r[0] for r in refs[4 * i:4 * i + 4]], axis=0) for i in range(4))
    low = lax.broadcasted_iota(jnp.int32, (2 * BLOCK, LANES), 1) < HEAD_DIM
    keep_lo = jnp.where(low, 1.0, 0.0).astype(BF16)
    keep_hi = jnp.where(low, 0.0, 1.0).astype(BF16)
    bias = bias_ref[0]
    bias2 = jnp.concatenate([bias, bias], axis=0)
    q = q_ref[0]
    group = (SWA_QH // SWA_KVH) * HEAD_DIM

    scores = []
    for h in range(SWA_KVH):
        q2 = jnp.concatenate([q[:, h * group:h * group + LANES],
                              q[:, h * group + LANES:(h + 1) * group]], axis=0)
        k_lo, k_hi = (kk, kr) if h == 0 else (kr, kk)
        scores.append(_dot_nt(q2 * keep_lo, k_lo) + bias2)
        scores.append(_dot_nt(q2 * keep_hi, k_hi) + bias2)

    sinks = [sink_ref[h, par][:, 0:1] for h in range(SWA_KVH) for par in range(2)]
    maxes = [jnp.maximum(jnp.max(s, axis=-1, keepdims=True), sk) for s, sk in zip(scores, sinks)]
    probs = [jnp.exp(s - m) for s, m in zip(scores, maxes)]
    dens = [jnp.sum(p, axis=-1, keepdims=True) + jnp.exp(sk - m) for p, sk, m in zip(probs, sinks, maxes)]
    vals = [vv, vr, vr, vv]
    outs = [_dot(p.astype(BF16), v_sel) / den for p, v_sel, den in zip(probs, vals, dens)]

    for h in range(SWA_KVH):
        o_lo, o_hi = outs[2 * h], outs[2 * h + 1]
        o2 = jnp.where(low, o_lo, o_hi)
        o_ref[0, :, h * group:h * group + LANES] = o2[0:BLOCK]
        o_ref[0, :, h * group + LANES:(h + 1) * group] = o2[BLOCK:2 * BLOCK]


def _attn_bias(nb):
    qi = np.arange(BLOCK)[:, None]
    kj = np.arange(3 * BLOCK)[None, :]
    near = np.abs(kj - BLOCK - qi) <= WINDOW
    blk = kj // BLOCK
    out = np.zeros((5, BLOCK, N_KEYS), np.float32)
    for var in range(4):
        ok = near & ((blk != 0) | bool(var & 2)) & ((blk != 2) | bool(var & 1))
        out[var, :, :3 * BLOCK] = np.where(ok, 0.0, NEG_INF)
    out[4, :, :3 * BLOCK] = NEG_INF
    return jnp.asarray(out)


def _attention(q, kk, kr, vv, vr, sink_tab):
    bsz, t, qw = q.shape
    nb = t // BLOCK
    cblk = CTX_LEN // BLOCK
    kv_w = SWA_KVH * HEAD_DIM
    at = lambda off: pl.BlockSpec((1, BLOCK, kv_w), lambda b, u: (b, jnp.clip(u + off, 0, nb - 1), 0))
    ctx = pl.BlockSpec((1, CTX_LEN, kv_w), lambda b, u: (b, 0, 0))
    keys = [at(-1), at(0), at(1), ctx]

    def variant(b, u):
        prev_ok = (u - 1 >= cblk).astype(jnp.int32)
        next_ok = (u + 1 <= nb - 1).astype(jnp.int32)
        return (jnp.where(u < cblk, 4, 2 * prev_ok + next_ok), 0, 0)

    return pl.pallas_call(
        _attn_kernel,
        grid=(bsz, nb),
        in_specs=[pl.BlockSpec((1, BLOCK, qw), lambda b, u: (b, u, 0)),
                  pl.BlockSpec((1, BLOCK, N_KEYS), variant),
                  pl.BlockSpec(sink_tab.shape, lambda b, u: (0, 0, 0, 0))] + keys * 4,
        out_specs=pl.BlockSpec((1, BLOCK, qw), lambda b, u: (b, u, 0)),
        out_shape=jax.ShapeDtypeStruct((bsz, t, qw), F32),
        compiler_params=_params(("arbitrary", "arbitrary")),
        name="swa_attn",
    )(q, _attn_bias(nb), sink_tab, *([kk] * 4 + [kr] * 4 + [vv] * 4 + [vr] * 4))


def _head_norm(o, g_row):
    outs = []
    for hd in range(o.shape[1] // LANES):
        oh = o[:, hd * LANES:(hd + 1) * LANES]
        ms = jnp.mean(oh * oh, axis=-1, keepdims=True)
        outs.append(oh * lax.rsqrt(ms + EPS) * g_row)
    return jnp.concatenate(outs, axis=1)


def _finish(x_ref, mod_ref, y, w_ref, fg_ref, o_ref, final, row):
    d = D_MODEL
    gate = mod_ref[pl.ds(row, 1), :][:, 2 * d:3 * d]
    xn = x_ref[0] + gate * _dot(y.astype(BF16), w_ref[...])
    if final:
        ms = jnp.mean(xn * xn, axis=-1, keepdims=True)
        xn = xn * lax.rsqrt(ms + EPS) * fg_ref[...]
    o_ref[0] = xn


def _out_even_kernel(x_ref, mod_ref, of_ref, ob_ref, gg_ref, hf_ref, hb_ref, gl_ref, ng_ref, w_ref, fg_ref,
                     o_ref, *, final):
    row = pl.program_id(0) if final else jnp.where(pl.program_id(1) == 0, CTX_ROW, pl.program_id(0))
    gla = _head_norm(of_ref[0] + ob_ref[0], ng_ref[...]) * _silu(gg_ref[0])
    lru = (hf_ref[0] + hb_ref[0]) * _silu(gl_ref[0])
    _finish(x_ref, mod_ref, jnp.concatenate([gla, lru], axis=1), w_ref, fg_ref, o_ref, final, row)


def _out_odd_kernel(x_ref, mod_ref, att_ref, gs_ref, of_ref, ob_ref, gr_ref, ng_ref, w_ref, fg_ref,
                    o_ref, *, final):
    row = pl.program_id(0) if final else jnp.where(pl.program_id(1) == 0, CTX_ROW, pl.program_id(0))
    att = att_ref[0] * _silu(gs_ref[0])
    ret = _head_norm(of_ref[0] + ob_ref[0], ng_ref[...]) * _silu(gr_ref[0])
    _finish(x_ref, mod_ref, jnp.concatenate([att, ret], axis=1), w_ref, fg_ref, o_ref, final, row)


def _out_proj(body, xs, mod, tok_inputs, ng, w_out, fg, final):
    bsz, t, d = xs.shape
    nt = t // TM
    skip = 1 if final else 0
    tok = lambda w: pl.BlockSpec((1, TM, w), lambda b, j: (b, j + skip, 0))
    full = lambda a: pl.BlockSpec(a.shape, lambda b, j: (0,) * a.ndim)
    return pl.pallas_call(
        functools.partial(body, final=final),
        grid=(bsz, nt - skip),
        in_specs=[tok(d), full(mod)] + [tok(a.shape[-1]) for a in tok_inputs]
                 + [full(ng), full(w_out), full(fg)],
        out_specs=pl.BlockSpec((1, TM, d), lambda b, j: (b, j, 0)),
        out_shape=jax.ShapeDtypeStruct((bsz, t - skip * TM, d), F32),
        compiler_params=_params(("arbitrary", "arbitrary")),
        name="out_proj",
    )(xs, mod, *tok_inputs, ng, w_out, fg)


def _block_diag_gates(wa, wx):
    lb = wa.shape[-1]
    z = jnp.zeros((lb, lb), wa.dtype)

    def pair(w, i):
        return jnp.concatenate([jnp.concatenate([w[2 * i], z], axis=1),
                                jnp.concatenate([z, w[2 * i + 1]], axis=1)], axis=0)

    return jnp.stack([jnp.concatenate([pair(wa, i), pair(wx, i)], axis=1) for i in range(2)]).astype(BF16)


def _rope_tables(n_tok):
    rows = n_tok // GRID_W
    row = jnp.repeat(jnp.arange(rows), GRID_W)
    col = jnp.tile(jnp.arange(GRID_W), rows)
    n_freq = HEAD_DIM // 4
    inv = ROPE_BASE ** (-jnp.arange(n_freq, dtype=F32) / n_freq)
    ang = jnp.concatenate([row.astype(F32)[:, None] * inv[None],
                           col.astype(F32)[:, None] * inv[None]], axis=-1)
    cos, sin = jnp.cos(ang), jnp.sin(ang)
    reps = LANES // HEAD_DIM
    cos_f = jnp.tile(cos, (1, 2 * reps))
    sin_s = jnp.tile(jnp.concatenate([-sin, sin], axis=-1), (1, reps))
    cos_f = jnp.concatenate([jnp.ones((CTX_LEN, LANES), F32), cos_f], axis=0)
    sin_s = jnp.concatenate([jnp.zeros((CTX_LEN, LANES), F32), sin_s], axis=0)
    return cos_f, sin_s


def kernel(x, c, ctx, c_ctx, ada_w, ada_b, norm_g, e_w_in, gla_up_fw, gla_b_fw, gla_up_bw, gla_b_bw, gla_norm_g, lru_conv_w, lru_conv_b, lru_wa_fw, lru_ba_fw, lru_wx_fw, lru_bx_fw, lru_lam_fw, lru_wa_bw, lru_ba_bw, lru_wx_bw, lru_bx_bw, lru_lam_bw, e_w_out, o_w_in, swa_sink, ret_dec_fw, ret_dec_bw, ret_norm_g, o_w_out, final_g):
    bsz, n_tok, d = x.shape
    depth = ada_w.shape[0]
    assert d == D_MODEL and ctx.shape[1] == CTX_LEN and n_tok % TM == 0 and bsz <= CTX_ROW

    cvec = jnp.zeros((SUBLANES, d), F32).at[0:bsz].set(c).at[CTX_ROW].set(c_ctx)
    mods = _ada_mod(cvec, ada_w, ada_b)
    xs = jnp.concatenate([ctx, x], axis=1)
    cos_f, sin_s = _rope_tables(n_tok)
    row = lambda a: a.reshape(1, -1)
    fg = row(final_g)

    for i in range(depth):
        jj = i // 2
        final = i == depth - 1
        mod = mods[i]
        g = row(norm_g[i])
        if i % 2 == 0:
            w = e_w_in[jj]
            w_all = jnp.concatenate(
                [w[:, 0:1536], w[:, 1568:2592], w[:, 1536:1568], jnp.zeros((d, LANES - 2 * GLA_RANK), F32)],
                axis=1).astype(BF16)
            up_pad = jnp.zeros((LANES, 2 * QK_W), F32)
            up_pad = up_pad.at[0:GLA_RANK, 0:QK_W].set(gla_up_fw[jj])
            up_pad = up_pad.at[GLA_RANK:2 * GLA_RANK, QK_W:].set(gla_up_bw[jj]).astype(BF16)
            ub = jnp.concatenate([gla_b_fw[jj], gla_b_bw[jj]]).reshape(1, -1)
            q, k, v, gg, la_f, la_b, xr, gl = _proj_even(xs, mod, g, w_all, up_pad, ub)
            o_f, o_b = _bidir_scan(q, k, v, la_f=la_f, la_b=la_b)
            h_f, h_b = _lru_scan(
                xr, lru_conv_w[jj], row(lru_conv_b[jj]),
                _block_diag_gates(lru_wa_fw[jj], lru_wx_fw[jj]), row(lru_ba_fw[jj]), row(lru_bx_fw[jj]),
                row(lru_lam_fw[jj]),
                _block_diag_gates(lru_wa_bw[jj], lru_wx_bw[jj]), row(lru_ba_bw[jj]), row(lru_bx_bw[jj]),
                row(lru_lam_bw[jj]))
            xs = _out_proj(_out_even_kernel, xs, mod, (o_f, o_b, gg, h_f, h_b, gl),
                           row(gla_norm_g[jj]), e_w_out[jj].astype(BF16), fg, final)
        else:
            q, kk, kr, vv, vr, gs, rq, rk, rv, gr = _proj_odd(
                xs, mod, g, o_w_in[jj].astype(BF16), cos_f, sin_s)
            sink = swa_sink[jj].reshape(SWA_KVH, 2, 2)
            sink_tab = jnp.broadcast_to(
                jnp.transpose(sink, (0, 2, 1))[:, :, :, None, None],
                (SWA_KVH, 2, 2, BLOCK, LANES)).reshape(SWA_KVH, 2, 2 * BLOCK, LANES)
            att = _attention(q, kk, kr, vv, vr, sink_tab)
            ld_f = row(jnp.repeat(ret_dec_fw[jj], RET_DK))
            ld_b = row(jnp.repeat(ret_dec_bw[jj], RET_DK))
            o_f, o_b = _bidir_scan(rq, rk, rv, ld_f=ld_f, ld_b=ld_b)
            xs = _out_proj(_out_odd_kernel, xs, mod, (att, gs, o_f, o_b, gr),
                           row(ret_norm_g[jj]), o_w_out[jj].astype(BF16), fg, final)
    return xs
```

```python
import functools

import numpy as np
import jax
import jax.numpy as jnp
from jax import lax
from jax.experimental import pallas as pl
from jax.experimental.pallas import tpu as pltpu

F32 = jnp.float32
BF16 = jnp.bfloat16

D_MODEL = 1024
CTX_LEN = 256
GRID_W = 64
EPS = 1e-6
NEG_INF = -1e30
ROPE_BASE = 10000.0
CHUNK = 64

GLA_HEADS = 4
GLA_DK = 64
GLA_DV = 128
GLA_RANK = 16
GLA_TEMP = 16.0
LRU_WIDTH = 512
LRU_BLOCKS = 4
LRU_CONV = 4
LRU_C = 8.0
HEAD_DIM = 64
SWA_QH = 8
SWA_KVH = 2
WINDOW = 128
BLOCK = 128
RET_HEADS = 4
RET_DK = 64
RET_DV = 128

LANES = 128
SUBLANES = 8
PACKED_ROWS = 16
TM = 256
CTX_ROW = SUBLANES - 1
QK_W = GLA_HEADS * GLA_DK
V_W = GLA_HEADS * GLA_DV
QKV_W = 2 * QK_W + V_W
VMEM_LIMIT = 56 * 1024 * 1024

EVEN_GG, EVEN_XR, EVEN_GL, EVEN_COLS = 2, 3, 4, 2560
ODD_Q, ODD_KV, ODD_GS, ODD_GR, ODD_COLS = 2, 3, 4, 5, 3072


def _params(sem):
    return pltpu.CompilerParams(dimension_semantics=sem, vmem_limit_bytes=VMEM_LIMIT)


def _dot(a, b):
    return jnp.dot(a, b, preferred_element_type=F32)


def _dot_nt(a, b):
    return lax.dot_general(a, b, (((1,), (1,)), ((), ())), preferred_element_type=F32)


def _dot_tn(a, b):
    return lax.dot_general(a, b, (((0,), (0,)), ((), ())), preferred_element_type=F32)


def _sigmoid(x):
    return 1.0 / (1.0 + jnp.exp(-x))


def _silu(x):
    return x * _sigmoid(x)


def _log_sigmoid(x):
    return jnp.minimum(x, 0.0) - jnp.log(1.0 + jnp.exp(-jnp.abs(x)))


def _proj_tile(t):
    n = t // TM
    for k in (5, 4, 3, 2, 1):
        if n % k == 0:
            return k * TM
    return TM


def _ada_kernel(c_ref, w_ref, b_ref, o_ref):
    s = _silu(c_ref[...])
    o_ref[0] = _dot(s.astype(BF16), w_ref[0].astype(BF16)) + b_ref[0]


def _ada_mod(cvec, ada_w, ada_b):
    depth, d, d3 = ada_w.shape
    nsplit = d3 // d
    return pl.pallas_call(
        _ada_kernel,
        grid=(depth, nsplit),
        in_specs=[
            pl.BlockSpec((SUBLANES, d), lambda i, n: (0, 0)),
            pl.BlockSpec((1, d, d), lambda i, n: (i, 0, n)),
            pl.BlockSpec((1, 1, d), lambda i, n: (i, 0, n)),
        ],
        out_specs=pl.BlockSpec((1, SUBLANES, d), lambda i, n: (i, 0, n)),
        out_shape=jax.ShapeDtypeStruct((depth, SUBLANES, d3), F32),
        compiler_params=_params(("arbitrary", "arbitrary")),
        name="ada_mod",
    )(cvec, ada_w, ada_b.reshape(depth, 1, d3))


def _mod_row(mod_ref, sub, skip):
    if skip:
        return mod_ref[pl.ds(pl.program_id(0), 1), :]
    is_ctx = jnp.logical_and(pl.program_id(1) == 0, sub == 0)
    return mod_ref[pl.ds(jnp.where(is_ctx, CTX_ROW, pl.program_id(0)), 1), :]


def _modulated(x, m, g_row):
    d = D_MODEL
    ms = jnp.mean(x * x, axis=-1, keepdims=True)
    return (x * lax.rsqrt(ms + EPS)) * g_row * (1.0 + m[:, d:2 * d]) + m[:, 0:d]


def _proj_even_kernel(x_ref, mod_ref, g_ref, w_ref, up_ref, ub_ref, z_ref, la_ref, *, pieces):
    for sub in range(pieces):
        rows = slice(sub * TM, (sub + 1) * TM)
        h = _modulated(x_ref[0, rows, :], _mod_row(mod_ref, sub, False), g_ref[...])
        z = _dot(h.astype(BF16), w_ref[...])
        z_ref[0, rows, 0:QK_W] = (z[:, 0:QK_W] * (GLA_DK ** -0.5)).astype(BF16)
        z_ref[0, rows, QK_W:EVEN_COLS] = z[:, QK_W:EVEN_COLS].astype(BF16)
        pre = _dot(z[:, EVEN_COLS:EVEN_COLS + LANES].astype(BF16), up_ref[...]) + ub_ref[...]
        la_ref[0, rows, :] = _log_sigmoid(pre) * (1.0 / GLA_TEMP)


def _proj_even(xs, mod, g, w_all, up_pad, ub):
    bsz, t, d = xs.shape
    tp = _proj_tile(t)
    tok = lambda w: pl.BlockSpec((1, tp, w), lambda b, j: (b, j, 0))
    full = lambda a: pl.BlockSpec(a.shape, lambda b, j: (0,) * a.ndim)
    return pl.pallas_call(
        functools.partial(_proj_even_kernel, pieces=tp // TM),
        grid=(bsz, t // tp),
        in_specs=[tok(d), full(mod), full(g), full(w_all), full(up_pad), full(ub)],
        out_specs=[tok(EVEN_COLS), tok(2 * QK_W)],
        out_shape=[jax.ShapeDtypeStruct((bsz, t, EVEN_COLS), BF16),
                   jax.ShapeDtypeStruct((bsz, t, 2 * QK_W), F32)],
        compiler_params=_params(("arbitrary", "arbitrary")),
        name="proj_even",
    )(xs, mod, g, w_all, up_pad, ub)


def _rope(x, cos_f, sin_s, lane):
    outs = []
    for g in range(x.shape[1] // LANES):
        xg = x[:, g * LANES:(g + 1) * LANES]
        up = pltpu.roll(xg, LANES - HEAD_DIM // 2, axis=1)
        dn = pltpu.roll(xg, HEAD_DIM // 2, axis=1)
        swapped = jnp.where(lane % HEAD_DIM < HEAD_DIM // 2, up, dn)
        outs.append(xg * cos_f + swapped * sin_s)
    return outs[0] if len(outs) == 1 else jnp.concatenate(outs, axis=1)


def _proj_odd_kernel(x_ref, mod_ref, g_ref, w_ref, cos_ref, sin_ref, z_ref, *, pieces):
    lane = lax.broadcasted_iota(jnp.int32, (TM, LANES), 1)
    for sub in range(pieces):
        rows = slice(sub * TM, (sub + 1) * TM)
        h = _modulated(x_ref[0, rows, :], _mod_row(mod_ref, sub, False), g_ref[...])
        z = _dot(h.astype(BF16), w_ref[...])
        cos_f = cos_ref[rows, :]
        sin_s = sin_ref[rows, :]
        rope = lambda a: _rope(a, cos_f, sin_s, lane)
        k = rope(z[:, 512:640])
        v = z[:, 640:768]
        c0 = ODD_Q * 512
        z_ref[0, rows, 0:QK_W] = rope(z[:, 1280:1536]).astype(BF16)
        z_ref[0, rows, QK_W:2 * QK_W] = (rope(z[:, 1536:1792]) * (RET_DK ** -0.5)).astype(BF16)
        z_ref[0, rows, 2 * QK_W:QKV_W] = z[:, 1792:2304].astype(BF16)
        z_ref[0, rows, c0:c0 + 512] = (rope(z[:, 0:512]) * (HEAD_DIM ** -0.5)).astype(BF16)
        z_ref[0, rows, c0 + 512:c0 + 640] = k.astype(BF16)
        z_ref[0, rows, c0 + 640:c0 + 768] = pltpu.roll(k, HEAD_DIM, axis=1).astype(BF16)
        z_ref[0, rows, c0 + 768:c0 + 896] = v.astype(BF16)
        z_ref[0, rows, c0 + 896:c0 + 1024] = pltpu.roll(v, HEAD_DIM, axis=1).astype(BF16)
        z_ref[0, rows, ODD_GS * 512:(ODD_GS + 1) * 512] = z[:, 768:1280].astype(BF16)
        z_ref[0, rows, ODD_GR * 512:(ODD_GR + 1) * 512] = z[:, 2304:2816].astype(BF16)


def _proj_odd(xs, mod, g, w_in, cos_f, sin_s):
    bsz, t, d = xs.shape
    tp = _proj_tile(t)
    tok = lambda w: pl.BlockSpec((1, tp, w), lambda b, j: (b, j, 0))
    full = lambda a: pl.BlockSpec(a.shape, lambda b, j: (0,) * a.ndim)
    tab = pl.BlockSpec((tp, LANES), lambda b, j: (j, 0))
    return pl.pallas_call(
        functools.partial(_proj_odd_kernel, pieces=tp // TM),
        grid=(bsz, t // tp),
        in_specs=[tok(d), full(mod), full(g), full(w_in), tab, tab],
        out_specs=tok(ODD_COLS),
        out_shape=jax.ShapeDtypeStruct((bsz, t, ODD_COLS), BF16),
        compiler_params=_params(("arbitrary", "arbitrary")),
        name="proj_odd",
    )(xs, mod, g, w_in, cos_f, sin_s)


def _bwd_tile(j, nt):
    return jnp.where(j == 0, 0, nt - j)


def _scan_one_direction(qkv, bcum, s_ref, o_ref, reverse):
    nch = TM // CHUNK
    col_k = lax.broadcasted_iota(jnp.int32, (CHUNK, QK_W), 1)
    row_k = lax.broadcasted_iota(jnp.int32, (CHUNK, QK_W), 0)
    col_v = lax.broadcasted_iota(jnp.int32, (CHUNK, V_W), 1)
    if reverse:
        tri = (col_k % CHUNK) >= row_k
    else:
        tri = (col_k % CHUNK) <= row_k
    bd = (lax.broadcasted_iota(jnp.int32, (V_W, QK_W), 0) // GLA_DV
          == lax.broadcasted_iota(jnp.int32, (V_W, QK_W), 1) // GLA_DK)
    for c in (range(nch - 1, -1, -1) if reverse else range(nch)):
        r0 = c * CHUNK
        b = bcum[r0:r0 + CHUNK]
        last = b[0:1] if reverse else b[CHUNK - 1:CHUNK]
        qc = qkv[r0:r0 + CHUNK, 0:QK_W].astype(F32)
        kc = qkv[r0:r0 + CHUNK, QK_W:2 * QK_W].astype(F32)
        vc = qkv[r0:r0 + CHUNK, 2 * QK_W:QKV_W].astype(F32)
        q_in = (qc * jnp.exp(b)).astype(BF16)
        k_in = kc * jnp.exp(-b)
        k_st = (kc * jnp.exp(last - b)).astype(BF16)
        decay = jnp.exp(last)
        k_bd = jnp.concatenate(
            [jnp.where(col_k // GLA_DK == h, k_in, 0.0) for h in range(GLA_HEADS)], axis=0).astype(BF16)
        v_bd = jnp.concatenate(
            [jnp.where(col_v // GLA_DV == h, vc, 0.0) for h in range(GLA_HEADS)], axis=0).astype(BF16)
        att = jnp.where(tri, _dot_nt(q_in, k_bd), 0.0)
        s_t = s_ref[...]
        o = _dot(att.astype(BF16), v_bd) + _dot_nt(q_in, s_t.astype(BF16))
        o_ref[0, r0:r0 + CHUNK, :] = o.astype(o_ref.dtype)
        d_st = _dot_tn(vc.astype(BF16), k_st)
        s_ref[...] = s_t * decay + jnp.where(bd, d_st, 0.0)


def _cumsum_chunks(la, tri_ref):
    hi = la.astype(BF16)
    lo = (la - hi.astype(F32)).astype(BF16)
    return _dot(tri_ref[...], hi) + _dot(tri_ref[...], lo)


def _gla_kernel(zf, laf, zb, lab, trif, trib, of_ref, ob_ref, sf_ref, sb_ref):
    @pl.when(pl.program_id(1) == 0)
    def _():
        sf_ref[...] = jnp.zeros_like(sf_ref)
        sb_ref[...] = jnp.zeros_like(sb_ref)

    _scan_one_direction(zf[0], _cumsum_chunks(laf[0], trif), sf_ref, of_ref, False)
    _scan_one_direction(zb[0], _cumsum_chunks(lab[0], trib), sb_ref, ob_ref, True)


def _ret_kernel(zf, zb, ldf, ldb, of_ref, ob_ref, sf_ref, sb_ref):
    @pl.when(pl.program_id(1) == 0)
    def _():
        sf_ref[...] = jnp.zeros_like(sf_ref)
        sb_ref[...] = jnp.zeros_like(sb_ref)

    pos = lax.broadcasted_iota(jnp.int32, (TM, QK_W), 0) % CHUNK
    bf = (pos + 1).astype(F32) * _log_sigmoid(ldf[...])
    bb = (CHUNK - pos).astype(F32) * _log_sigmoid(ldb[...])
    _scan_one_direction(zf[0], bf, sf_ref, of_ref, False)
    _scan_one_direction(zb[0], bb, sb_ref, ob_ref, True)


def _chunk_tri(reverse):
    t = np.arange(TM)
    same = (t[:, None] // CHUNK) == (t[None, :] // CHUNK)
    order = (t[None, :] >= t[:, None]) if reverse else (t[None, :] <= t[:, None])
    return jnp.asarray(same & order, dtype=BF16)


def _bidir_scan(z, la=None, ld_f=None, ld_b=None):
    bsz, t, _ = z.shape
    nt = t // TM
    fwd = lambda w, c: pl.BlockSpec((1, TM, w), lambda b, j: (b, j, c))
    bwd = lambda w, c: pl.BlockSpec((1, TM, w), lambda b, j: (b, _bwd_tile(j, nt), c))
    full = lambda a: pl.BlockSpec(a.shape, lambda b, j: (0,) * a.ndim)
    common = dict(
        grid=(bsz, nt),
        out_specs=[fwd(V_W, 0), bwd(V_W, 0)],
        out_shape=[jax.ShapeDtypeStruct((bsz, t, V_W), BF16)] * 2,
        scratch_shapes=[pltpu.VMEM((V_W, QK_W), F32)] * 2,
        compiler_params=_params(("arbitrary", "arbitrary")),
    )
    if la is not None:
        tri_f, tri_b = _chunk_tri(False), _chunk_tri(True)
        return pl.pallas_call(
            _gla_kernel,
            in_specs=[fwd(QKV_W, 0), fwd(QK_W, 0), bwd(QKV_W, 0), bwd(QK_W, 1), full(tri_f), full(tri_b)],
            name="gla_scan", **common,
        )(z, la, z, la, tri_f, tri_b)
    return pl.pallas_call(
        _ret_kernel,
        in_specs=[fwd(QKV_W, 0), bwd(QKV_W, 0), full(ld_f), full(ld_b)],
        name="ret_scan", **common,
    )(z, z, ld_f, ld_b)


def _lru_conv(x_ref, prev_ref, next_ref, cw_ref, cb_ref, tile, nt):
    prev_ok = tile >= 2
    next_ok = jnp.logical_and(tile >= 1, tile <= nt - 2)
    prev = jnp.where(prev_ok, prev_ref[0].astype(F32), 0.0)
    nxt = jnp.where(next_ok, next_ref[0].astype(F32), 0.0)
    ext = jnp.concatenate([prev, x_ref[0].astype(F32), nxt], axis=0)
    n = ext.shape[0]
    lo, hi = PACKED_ROWS, PACKED_ROWS + TM
    cw = cw_ref[...]
    acc = ext[lo:hi] * cw[2:3]
    acc += pltpu.roll(ext, 2, axis=0)[lo:hi] * cw[0:1]
    acc += pltpu.roll(ext, 1, axis=0)[lo:hi] * cw[1:2]
    acc += pltpu.roll(ext, n - 1, axis=0)[lo:hi] * cw[3:4]
    return acc + cb_ref[...]


def _lru_one_direction(xc, w_ref, ba_ref, bx_ref, lam_ref, carry_ref, o_ref, reverse):
    half = LRU_WIDTH // 2
    xcb = xc.astype(BF16)
    pre0 = _dot(xcb[:, 0:half], w_ref[0])
    pre1 = _dot(xcb[:, half:], w_ref[1])
    pa = jnp.concatenate([pre0[:, 0:half], pre1[:, 0:half]], axis=1) + ba_ref[...]
    px = jnp.concatenate([pre0[:, half:], pre1[:, half:]], axis=1) + bx_ref[...]
    r = _sigmoid(pa)
    ig = _sigmoid(px)
    log_a = (LRU_C * r) * _log_sigmoid(lam_ref[...])
    a = jnp.exp(log_a)
    u = jnp.sqrt(1.0 - a * a) * (ig * xc)

    ng = TM // SUBLANES
    a3 = a.reshape(ng, SUBLANES, LRU_WIDTH)
    u3 = u.reshape(ng, SUBLANES, LRU_WIDTH)
    sub = lax.broadcasted_iota(jnp.int32, (ng, SUBLANES, LRU_WIDTH), 1)
    k = 1
    while k < SUBLANES:
        if reverse:
            a_s = pltpu.roll(a3, SUBLANES - k, axis=1)
            u_s = pltpu.roll(u3, SUBLANES - k, axis=1)
            ok = sub < SUBLANES - k
        else:
            a_s = pltpu.roll(a3, k, axis=1)
            u_s = pltpu.roll(u3, k, axis=1)
            ok = sub >= k
        u3 = a3 * jnp.where(ok, u_s, 0.0) + u3
        a3 = a3 * jnp.where(ok, a_s, 1.0)
        k *= 2

    edge = 0 if reverse else SUBLANES - 1
    carry = carry_ref[...]
    starts = [None] * ng
    for g in (range(ng - 1, -1, -1) if reverse else range(ng)):
        starts[g] = carry
        carry = a3[g, edge:edge + 1, :] * carry + u3[g, edge:edge + 1, :]
    carry_ref[...] = carry
    h = jnp.concatenate([a3[g] * starts[g] + u3[g] for g in range(ng)], axis=0)
    o_ref[0] = h.astype(o_ref.dtype)


def _lru_kernel(xf, xfp, xfn, xb, xbp, xbn, cw, cb,
                wf, baf, bxf, lamf, wb, bab, bxb, lamb,
                hf_ref, hb_ref, cf_ref, cbk_ref, *, nt):
    j = pl.program_id(1)

    @pl.when(j == 0)
    def _():
        cf_ref[...] = jnp.zeros_like(cf_ref)
        cbk_ref[...] = jnp.zeros_like(cbk_ref)

    xc_f = _lru_conv(xf, xfp, xfn, cw, cb, j, nt)
    _lru_one_direction(xc_f, wf, baf, bxf, lamf, cf_ref, hf_ref, False)
    xc_b = _lru_conv(xb, xbp, xbn, cw, cb, _bwd_tile(j, nt), nt)
    _lru_one_direction(xc_b, wb, bab, bxb, lamb, cbk_ref, hb_ref, True)


def _lru_scan(z, cw, cb, wf, baf, bxf, lamf, wb, bab, bxb, lamb):
    bsz, t, _ = z.shape
    w = LRU_WIDTH
    nt = t // TM
    per = TM // PACKED_ROWS
    nrow = t // PACKED_ROWS
    f_t = lambda j: j
    b_t = lambda j: _bwd_tile(j, nt)

    def specs(tile_of):
        return [
            pl.BlockSpec((1, TM, w), lambda b, j: (b, tile_of(j), EVEN_XR)),
            pl.BlockSpec((1, PACKED_ROWS, w),
                         lambda b, j: (b, jnp.maximum(tile_of(j) * per - 1, 0), EVEN_XR)),
            pl.BlockSpec((1, PACKED_ROWS, w),
                         lambda b, j: (b, jnp.minimum((tile_of(j) + 1) * per, nrow - 1), EVEN_XR)),
        ]

    full = lambda a: pl.BlockSpec(a.shape, lambda b, j: (0,) * a.ndim)
    consts = (cw, cb, wf, baf, bxf, lamf, wb, bab, bxb, lamb)
    return pl.pallas_call(
        functools.partial(_lru_kernel, nt=nt),
        grid=(bsz, nt),
        in_specs=specs(f_t) + specs(b_t) + [full(a) for a in consts],
        out_specs=[pl.BlockSpec((1, TM, w), lambda b, j: (b, j, 0)),
                   pl.BlockSpec((1, TM, w), lambda b, j: (b, b_t(j), 0))],
        out_shape=[jax.ShapeDtypeStruct((bsz, t, w), BF16)] * 2,
        scratch_shapes=[pltpu.VMEM((1, w), F32)] * 2,
        compiler_params=_params(("arbitrary", "arbitrary")),
        name="lru_scan",
    )(z, z, z, z, z, z, *consts)


N_KEYS = 3 * BLOCK + CTX_LEN


def _attn_kernel(q_ref, bias_ref, sink_ref, kp_ref, kc_ref, kn_ref, kx_ref, o_ref):
    kv = jnp.concatenate([kp_ref[0], kc_ref[0], kn_ref[0], kx_ref[0]], axis=0)
    kk, kr, vv, vr = (kv[:, i * LANES:(i + 1) * LANES] for i in range(4))
    low = lax.broadcasted_iota(jnp.int32, (2 * BLOCK, LANES), 1) < HEAD_DIM
    keep_lo = jnp.where(low, 1.0, 0.0).astype(BF16)
    keep_hi = jnp.where(low, 0.0, 1.0).astype(BF16)
    bias = bias_ref[0]
    bias2 = jnp.concatenate([bias, bias], axis=0)
    q = q_ref[0]
    group = (SWA_QH // SWA_KVH) * HEAD_DIM

    scores = []
    for h in range(SWA_KVH):
        q2 = jnp.concatenate([q[:, h * group:h * group + LANES],
                              q[:, h * group + LANES:(h + 1) * group]], axis=0)
        k_lo, k_hi = (kk, kr) if h == 0 else (kr, kk)
        scores.append(_dot_nt(q2 * keep_lo, k_lo) + bias2)
        scores.append(_dot_nt(q2 * keep_hi, k_hi) + bias2)

    sinks = [sink_ref[h, par][:, 0:1] for h in range(SWA_KVH) for par in range(2)]
    maxes = [jnp.maximum(jnp.max(s, axis=-1, keepdims=True), sk) for s, sk in zip(scores, sinks)]
    probs = [jnp.exp(s - m) for s, m in zip(scores, maxes)]
    dens = [jnp.sum(p, axis=-1, keepdims=True) + jnp.exp(sk - m) for p, sk, m in zip(probs, sinks, maxes)]
    vals = [vv, vr, vr, vv]
    outs = [_dot(p.astype(BF16), v_sel) / den for p, v_sel, den in zip(probs, vals, dens)]

    for h in range(SWA_KVH):
        o2 = jnp.where(low, outs[2 * h], outs[2 * h + 1]).astype(o_ref.dtype)
        o_ref[0, :, h * group:h * group + LANES] = o2[0:BLOCK]
        o_ref[0, :, h * group + LANES:(h + 1) * group] = o2[BLOCK:2 * BLOCK]


def _attn_bias():
    qi = np.arange(BLOCK)[:, None]
    kj = np.arange(3 * BLOCK)[None, :]
    near = np.abs(kj - BLOCK - qi) <= WINDOW
    blk = kj // BLOCK
    out = np.zeros((5, BLOCK, N_KEYS), np.float32)
    for var in range(4):
        ok = near & ((blk != 0) | bool(var & 2)) & ((blk != 2) | bool(var & 1))
        out[var, :, :3 * BLOCK] = np.where(ok, 0.0, NEG_INF)
    out[4, :, :3 * BLOCK] = NEG_INF
    return jnp.asarray(out)


def _attention(z, sink_tab):
    bsz, t, _ = z.shape
    qw = SWA_QH * HEAD_DIM
    nb = t // BLOCK
    cblk = CTX_LEN // BLOCK
    at = lambda off: pl.BlockSpec((1, BLOCK, 4 * LANES), lambda b, u: (b, jnp.clip(u + off, 0, nb - 1), ODD_KV))
    ctx = pl.BlockSpec((1, CTX_LEN, 4 * LANES), lambda b, u: (b, 0, ODD_KV))

    def variant(b, u):
        prev_ok = (u - 1 >= cblk).astype(jnp.int32)
        next_ok = (u + 1 <= nb - 1).astype(jnp.int32)
        return (jnp.where(u < cblk, 4, 2 * prev_ok + next_ok), 0, 0)

    return pl.pallas_call(
        _attn_kernel,
        grid=(bsz, nb),
        in_specs=[pl.BlockSpec((1, BLOCK, qw), lambda b, u: (b, u, ODD_Q)),
                  pl.BlockSpec((1, BLOCK, N_KEYS), variant),
                  pl.BlockSpec(sink_tab.shape, lambda b, u: (0, 0, 0, 0)),
                  at(-1), at(0), at(1), ctx],
        out_specs=pl.BlockSpec((1, BLOCK, qw), lambda b, u: (b, u, 0)),
        out_shape=jax.ShapeDtypeStruct((bsz, t, qw), BF16),
        compiler_params=_params(("arbitrary", "arbitrary")),
        name="swa_attn",
    )(z, _attn_bias(), sink_tab, z, z, z, z)


def _head_norm(o, g_row):
    outs = []
    for hd in range(o.shape[1] // LANES):
        oh = o[:, hd * LANES:(hd + 1) * LANES]
        ms = jnp.mean(oh * oh, axis=-1, keepdims=True)
        outs.append(oh * lax.rsqrt(ms + EPS) * g_row)
    return jnp.concatenate(outs, axis=1)


def _finish(x, gate, y, w_ref, fg_ref, final):
    xn = x + gate * _dot(y.astype(BF16), w_ref[...])
    if final:
        ms = jnp.mean(xn * xn, axis=-1, keepdims=True)
        xn = xn * lax.rsqrt(ms + EPS) * fg_ref[...]
    return xn


def _out_even_kernel(x_ref, mod_ref, of_ref, ob_ref, gg_ref, hf_ref, hb_ref, gl_ref, ng_ref, w_ref, fg_ref,
                     o_ref, *, final, pieces):
    d = D_MODEL
    for sub in range(pieces):
        rows = slice(sub * TM, (sub + 1) * TM)
        ld = lambda r: r[0, rows, :].astype(F32)
        gla = _head_norm(ld(of_ref) + ld(ob_ref), ng_ref[...]) * _silu(ld(gg_ref))
        lru = (ld(hf_ref) + ld(hb_ref)) * _silu(ld(gl_ref))
        gate = _mod_row(mod_ref, sub, final)[:, 2 * d:3 * d]
        o_ref[0, rows, :] = _finish(x_ref[0, rows, :], gate, jnp.concatenate([gla, lru], axis=1),
                                    w_ref, fg_ref, final)


def _out_odd_kernel(x_ref, mod_ref, att_ref, gs_ref, of_ref, ob_ref, gr_ref, ng_ref, w_ref, fg_ref,
                    o_ref, *, final, pieces):
    d = D_MODEL
    for sub in range(pieces):
        rows = slice(sub * TM, (sub + 1) * TM)
        ld = lambda r: r[0, rows, :].astype(F32)
        att = ld(att_ref) * _silu(ld(gs_ref))
        ret = _head_norm(ld(of_ref) + ld(ob_ref), ng_ref[...]) * _silu(ld(gr_ref))
        gate = _mod_row(mod_ref, sub, final)[:, 2 * d:3 * d]
        o_ref[0, rows, :] = _finish(x_ref[0, rows, :], gate, jnp.concatenate([att, ret], axis=1),
                                    w_ref, fg_ref, final)


def _out_proj(body, xs, mod, tok_inputs, ng, w_out, fg, final):
    bsz, t, d = xs.shape
    tp = TM if final else _proj_tile(t)
    skip = 1 if final else 0
    tok = lambda w, c: pl.BlockSpec((1, tp, w), lambda b, j: (b, j + skip, c))
    full = lambda a: pl.BlockSpec(a.shape, lambda b, j: (0,) * a.ndim)
    return pl.pallas_call(
        functools.partial(body, final=final, pieces=tp // TM),
        grid=(bsz, t // tp - skip),
        in_specs=[tok(d, 0), full(mod)] + [tok(V_W, c) for _, c in tok_inputs]
                 + [full(ng), full(w_out), full(fg)],
        out_specs=pl.BlockSpec((1, tp, d), lambda b, j: (b, j, 0)),
        out_shape=jax.ShapeDtypeStruct((bsz, t - skip * TM, d), F32),
        compiler_params=_params(("arbitrary", "arbitrary")),
        name="out_proj",
    )(xs, mod, *[a for a, _ in tok_inputs], ng, w_out, fg)


def _block_diag_gates(wa, wx):
    lb = wa.shape[-1]
    z = jnp.zeros((lb, lb), wa.dtype)

    def pair(w, i):
        return jnp.concatenate([jnp.concatenate([w[2 * i], z], axis=1),
                                jnp.concatenate([z, w[2 * i + 1]], axis=1)], axis=0)

    return jnp.stack([jnp.concatenate([pair(wa, i), pair(wx, i)], axis=1) for i in range(2)]).astype(BF16)


def _rope_tables(n_tok):
    rows = n_tok // GRID_W
    row = jnp.repeat(jnp.arange(rows), GRID_W)
    col = jnp.tile(jnp.arange(GRID_W), rows)
    n_freq = HEAD_DIM // 4
    inv = ROPE_BASE ** (-jnp.arange(n_freq, dtype=F32) / n_freq)
    ang = jnp.concatenate([row.astype(F32)[:, None] * inv[None],
                           col.astype(F32)[:, None] * inv[None]], axis=-1)
    cos, sin = jnp.cos(ang), jnp.sin(ang)
    reps = LANES // HEAD_DIM
    cos_f = jnp.tile(cos, (1, 2 * reps))
    sin_s = jnp.tile(jnp.concatenate([-sin, sin], axis=-1), (1, reps))
    cos_f = jnp.concatenate([jnp.ones((CTX_LEN, LANES), F32), cos_f], axis=0)
    sin_s = jnp.concatenate([jnp.zeros((CTX_LEN, LANES), F32), sin_s], axis=0)
    return cos_f, sin_s


def kernel(x, c, ctx, c_ctx, ada_w, ada_b, norm_g, e_w_in, gla_up_fw, gla_b_fw, gla_up_bw, gla_b_bw, gla_norm_g, lru_conv_w, lru_conv_b, lru_wa_fw, lru_ba_fw, lru_wx_fw, lru_bx_fw, lru_lam_fw, lru_wa_bw, lru_ba_bw, lru_wx_bw, lru_bx_bw, lru_lam_bw, e_w_out, o_w_in, swa_sink, ret_dec_fw, ret_dec_bw, ret_norm_g, o_w_out, final_g):
    bsz, n_tok, d = x.shape
    depth = ada_w.shape[0]
    assert d == D_MODEL and ctx.shape[1] == CTX_LEN and n_tok % TM == 0 and bsz <= CTX_ROW

    cvec = jnp.zeros((SUBLANES, d), F32).at[0:bsz].set(c).at[CTX_ROW].set(c_ctx)
    mods = _ada_mod(cvec, ada_w, ada_b)
    xs = jnp.concatenate([ctx, x], axis=1)
    cos_f, sin_s = _rope_tables(n_tok)
    row = lambda a: a.reshape(1, -1)
    fg = row(final_g)

    for i in range(depth):
        jj = i // 2
        final = i == depth - 1
        mod = mods[i]
        g = row(norm_g[i])
        if i % 2 == 0:
            w = e_w_in[jj]
            w_all = jnp.concatenate(
                [w[:, 0:1536], w[:, 1568:2592], w[:, 1536:1568], jnp.zeros((d, LANES - 2 * GLA_RANK), F32)],
                axis=1).astype(BF16)
            up_pad = jnp.zeros((LANES, 2 * QK_W), F32)
            up_pad = up_pad.at[0:GLA_RANK, 0:QK_W].set(gla_up_fw[jj])
            up_pad = up_pad.at[GLA_RANK:2 * GLA_RANK, QK_W:].set(gla_up_bw[jj]).astype(BF16)
            ub = jnp.concatenate([gla_b_fw[jj], gla_b_bw[jj]]).reshape(1, -1)
            z, la = _proj_even(xs, mod, g, w_all, up_pad, ub)
            o_f, o_b = _bidir_scan(z, la=la)
            h_f, h_b = _lru_scan(
                z, lru_conv_w[jj], row(lru_conv_b[jj]),
                _block_diag_gates(lru_wa_fw[jj], lru_wx_fw[jj]), row(lru_ba_fw[jj]), row(lru_bx_fw[jj]),
                row(lru_lam_fw[jj]),
                _block_diag_gates(lru_wa_bw[jj], lru_wx_bw[jj]), row(lru_ba_bw[jj]), row(lru_bx_bw[jj]),
                row(lru_lam_bw[jj]))
            xs = _out_proj(_out_even_kernel, xs, mod,
                           ((o_f, 0), (o_b, 0), (z, EVEN_GG), (h_f, 0), (h_b, 0), (z, EVEN_GL)),
                           row(gla_norm_g[jj]), e_w_out[jj].astype(BF16), fg, final)
        else:
            z = _proj_odd(xs, mod, g, o_w_in[jj].astype(BF16), cos_f, sin_s)
            sink = swa_sink[jj].reshape(SWA_KVH, 2, 2)
            sink_tab = jnp.broadcast_to(
                jnp.transpose(sink, (0, 2, 1))[:, :, :, None, None],
                (SWA_KVH, 2, 2, BLOCK, LANES)).reshape(SWA_KVH, 2, 2 * BLOCK, LANES)
            att = _attention(z, sink_tab)
            ld_f = row(jnp.repeat(ret_dec_fw[jj], RET_DK))
            ld_b = row(jnp.repeat(ret_dec_bw[jj], RET_DK))
            o_f, o_b = _bidir_scan(z, ld_f=ld_f, ld_b=ld_b)
            xs = _out_proj(_out_odd_kernel, xs, mod,
                           ((att, 0), (z, ODD_GS), (o_f, 0), (o_b, 0), (z, ODD_GR)),
                           row(ret_norm_g[jj]), o_w_out[jj].astype(BF16), fg, final)
    return xs
```

```python
import functools

import numpy as np
import jax
import jax.numpy as jnp
from jax import lax
from jax.experimental import pallas as pl
from jax.experimental.pallas import tpu as pltpu

F32 = jnp.float32
BF16 = jnp.bfloat16

D_MODEL = 1024
CTX_LEN = 256
GRID_W = 64
EPS = 1e-6
NEG_INF = -1e30
ROPE_BASE = 10000.0
CHUNK = 64

GLA_HEADS = 4
GLA_DK = 64
GLA_DV = 128
GLA_RANK = 16
GLA_TEMP = 16.0
LRU_WIDTH = 512
LRU_BLOCKS = 4
LRU_CONV = 4
LRU_C = 8.0
HEAD_DIM = 64
SWA_QH = 8
SWA_KVH = 2
WINDOW = 128
BLOCK = 128
RET_HEADS = 4
RET_DK = 64
RET_DV = 128

LANES = 128
SUBLANES = 8
PACKED_ROWS = 16
TM = 256
CTX_ROW = SUBLANES - 1
QK_W = GLA_HEADS * GLA_DK
V_W = GLA_HEADS * GLA_DV
QKV_W = 2 * QK_W + V_W
VMEM_LIMIT = 56 * 1024 * 1024

EVEN_GG, EVEN_XR, EVEN_GL, EVEN_COLS = 2, 3, 4, 2560
ODD_Q, ODD_KV, ODD_GS, ODD_GR, ODD_COLS = 2, 3, 4, 5, 3072


def _params(sem):
    return pltpu.CompilerParams(dimension_semantics=sem, vmem_limit_bytes=VMEM_LIMIT)


def _dot(a, b):
    return jnp.dot(a, b, preferred_element_type=F32)


def _dot_nt(a, b):
    return lax.dot_general(a, b, (((1,), (1,)), ((), ())), preferred_element_type=F32)


def _dot_tn(a, b):
    return lax.dot_general(a, b, (((0,), (0,)), ((), ())), preferred_element_type=F32)


def _sigmoid(x):
    return 1.0 / (1.0 + jnp.exp(-x))


def _silu(x):
    return x * _sigmoid(x)


def _log_sigmoid(x):
    return jnp.minimum(x, 0.0) - jnp.log(1.0 + jnp.exp(-jnp.abs(x)))


def _proj_tile(t):
    n = t // TM
    for k in (5, 4, 3, 2, 1):
        if n % k == 0:
            return k * TM
    return TM


def _ada_kernel(c_ref, w_ref, b_ref, o_ref):
    s = _silu(c_ref[...])
    o_ref[0] = _dot(s.astype(BF16), w_ref[0].astype(BF16)) + b_ref[0]


def _ada_mod(cvec, ada_w, ada_b):
    depth, d, d3 = ada_w.shape
    nsplit = d3 // d
    return pl.pallas_call(
        _ada_kernel,
        grid=(depth, nsplit),
        in_specs=[
            pl.BlockSpec((SUBLANES, d), lambda i, n: (0, 0)),
            pl.BlockSpec((1, d, d), lambda i, n: (i, 0, n)),
            pl.BlockSpec((1, 1, d), lambda i, n: (i, 0, n)),
        ],
        out_specs=pl.BlockSpec((1, SUBLANES, d), lambda i, n: (i, 0, n)),
        out_shape=jax.ShapeDtypeStruct((depth, SUBLANES, d3), F32),
        compiler_params=_params(("arbitrary", "arbitrary")),
        name="ada_mod",
    )(cvec, ada_w, ada_b.reshape(depth, 1, d3))


def _mod_row(mod_ref, sub, skip):
    if skip:
        return mod_ref[pl.ds(pl.program_id(0), 1), :]
    is_ctx = jnp.logical_and(pl.program_id(1) == 0, sub == 0)
    return mod_ref[pl.ds(jnp.where(is_ctx, CTX_ROW, pl.program_id(0)), 1), :]


def _modulated(x, m, g_row):
    d = D_MODEL
    ms = jnp.mean(x * x, axis=-1, keepdims=True)
    return (x * lax.rsqrt(ms + EPS)) * g_row * (1.0 + m[:, d:2 * d]) + m[:, 0:d]


def _proj_even_kernel(x_ref, mod_ref, g_ref, w_ref, up_ref, ub_ref, z_ref, la_ref, *, pieces):
    for sub in range(pieces):
        rows = slice(sub * TM, (sub + 1) * TM)
        h = _modulated(x_ref[0, rows, :], _mod_row(mod_ref, sub, False), g_ref[...])
        z = _dot(h.astype(BF16), w_ref[...])
        z_ref[0, rows, 0:QK_W] = (z[:, 0:QK_W] * (GLA_DK ** -0.5)).astype(BF16)
        z_ref[0, rows, QK_W:EVEN_COLS] = z[:, QK_W:EVEN_COLS].astype(BF16)
        pre = _dot(z[:, EVEN_COLS:EVEN_COLS + LANES].astype(BF16), up_ref[...]) + ub_ref[...]
        la_ref[0, rows, :] = _log_sigmoid(pre) * (1.0 / GLA_TEMP)


def _proj_even(xs, mod, g, w_all, up_pad, ub):
    bsz, t, d = xs.shape
    tp = _proj_tile(t)
    tok = lambda w: pl.BlockSpec((1, tp, w), lambda b, j: (b, j, 0))
    full = lambda a: pl.BlockSpec(a.shape, lambda b, j: (0,) * a.ndim)
    return pl.pallas_call(
        functools.partial(_proj_even_kernel, pieces=tp // TM),
        grid=(bsz, t // tp),
        in_specs=[tok(d), full(mod), full(g), full(w_all), full(up_pad), full(ub)],
        out_specs=[tok(EVEN_COLS), tok(2 * QK_W)],
        out_shape=[jax.ShapeDtypeStruct((bsz, t, EVEN_COLS), BF16),
                   jax.ShapeDtypeStruct((bsz, t, 2 * QK_W), F32)],
        compiler_params=_params(("arbitrary", "arbitrary")),
        name="proj_even",
    )(xs, mod, g, w_all, up_pad, ub)


def _rope(x, cos_f, sin_s, lane):
    outs = []
    for g in range(x.shape[1] // LANES):
        xg = x[:, g * LANES:(g + 1) * LANES]
        up = pltpu.roll(xg, LANES - HEAD_DIM // 2, axis=1)
        dn = pltpu.roll(xg, HEAD_DIM // 2, axis=1)
        swapped = jnp.where(lane % HEAD_DIM < HEAD_DIM // 2, up, dn)
        outs.append(xg * cos_f + swapped * sin_s)
    return outs[0] if len(outs) == 1 else jnp.concatenate(outs, axis=1)


def _proj_odd_kernel(x_ref, mod_ref, g_ref, w_ref, cos_ref, sin_ref, z_ref, *, pieces):
    lane = lax.broadcasted_iota(jnp.int32, (TM, LANES), 1)
    for sub in range(pieces):
        rows = slice(sub * TM, (sub + 1) * TM)
        h = _modulated(x_ref[0, rows, :], _mod_row(mod_ref, sub, False), g_ref[...])
        z = _dot(h.astype(BF16), w_ref[...])
        cos_f = cos_ref[rows, :]
        sin_s = sin_ref[rows, :]
        rope = lambda a: _rope(a, cos_f, sin_s, lane)
        k = rope(z[:, 512:640])
        v = z[:, 640:768]
        c0 = ODD_Q * 512
        z_ref[0, rows, 0:QK_W] = rope(z[:, 1280:1536]).astype(BF16)
        z_ref[0, rows, QK_W:2 * QK_W] = (rope(z[:, 1536:1792]) * (RET_DK ** -0.5)).astype(BF16)
        z_ref[0, rows, 2 * QK_W:QKV_W] = z[:, 1792:2304].astype(BF16)
        z_ref[0, rows, c0:c0 + 512] = (rope(z[:, 0:512]) * (HEAD_DIM ** -0.5)).astype(BF16)
        z_ref[0, rows, c0 + 512:c0 + 640] = k.astype(BF16)
        z_ref[0, rows, c0 + 640:c0 + 768] = pltpu.roll(k, HEAD_DIM, axis=1).astype(BF16)
        z_ref[0, rows, c0 + 768:c0 + 896] = v.astype(BF16)
        z_ref[0, rows, c0 + 896:c0 + 1024] = pltpu.roll(v, HEAD_DIM, axis=1).astype(BF16)
        z_ref[0, rows, ODD_GS * 512:(ODD_GS + 1) * 512] = z[:, 768:1280].astype(BF16)
        z_ref[0, rows, ODD_GR * 512:(ODD_GR + 1) * 512] = z[:, 2304:2816].astype(BF16)


def _proj_odd(xs, mod, g, w_in, cos_f, sin_s):
    bsz, t, d = xs.shape
    tp = _proj_tile(t)
    tok = lambda w: pl.BlockSpec((1, tp, w), lambda b, j: (b, j, 0))
    full = lambda a: pl.BlockSpec(a.shape, lambda b, j: (0,) * a.ndim)
    tab = pl.BlockSpec((tp, LANES), lambda b, j: (j, 0))
    return pl.pallas_call(
        functools.partial(_proj_odd_kernel, pieces=tp // TM),
        grid=(bsz, t // tp),
        in_specs=[tok(d), full(mod), full(g), full(w_in), tab, tab],
        out_specs=tok(ODD_COLS),
        out_shape=jax.ShapeDtypeStruct((bsz, t, ODD_COLS), BF16),
        compiler_params=_params(("arbitrary", "arbitrary")),
        name="proj_odd",
    )(xs, mod, g, w_in, cos_f, sin_s)


def _bwd_tile(j, nt):
    return jnp.where(j == 0, 0, nt - j)


def _scan_both_directions(zf, bcum_f, zb, bcum_b, sf_ref, sb_ref, of_ref, ob_ref):
    nch = TM // CHUNK
    col_k = lax.broadcasted_iota(jnp.int32, (CHUNK, QK_W), 1)
    row_k = lax.broadcasted_iota(jnp.int32, (CHUNK, QK_W), 0)
    col_v = lax.broadcasted_iota(jnp.int32, (CHUNK, V_W), 1)
    tris = ((col_k % CHUNK) <= row_k, (col_k % CHUNK) >= row_k)
    bd = (lax.broadcasted_iota(jnp.int32, (V_W, QK_W), 0) // GLA_DV
          == lax.broadcasted_iota(jnp.int32, (V_W, QK_W), 1) // GLA_DK)
    dirs = ((zf, bcum_f, sf_ref, of_ref, False), (zb, bcum_b, sb_ref, ob_ref, True))
    preps, all_atts = [], []
    for step in range(nch):
        prep = []
        for qkv, bcum, _, _, reverse in dirs:
            r0 = (nch - 1 - step if reverse else step) * CHUNK
            b = bcum[r0:r0 + CHUNK]
            last = b[0:1] if reverse else b[CHUNK - 1:CHUNK]
            qc = qkv[r0:r0 + CHUNK, 0:QK_W].astype(F32)
            kc = qkv[r0:r0 + CHUNK, QK_W:2 * QK_W].astype(F32)
            vc = qkv[r0:r0 + CHUNK, 2 * QK_W:QKV_W]
            q_in = (qc * jnp.exp(b)).astype(BF16)
            k_in = kc * jnp.exp(-b)
            k_st = (kc * jnp.exp(last - b)).astype(BF16)
            k_bd = jnp.concatenate(
                [jnp.where(col_k // GLA_DK == h, k_in, 0.0) for h in range(GLA_HEADS)], axis=0).astype(BF16)
            v_bd = jnp.concatenate(
                [jnp.where(col_v // GLA_DV == h, vc, jnp.zeros_like(vc)) for h in range(GLA_HEADS)], axis=0)
            prep.append((r0, q_in, k_bd, v_bd, vc, k_st, jnp.exp(last)))
        preps.append(prep)
        all_atts.append([_dot_nt(q_in, k_bd) for _, q_in, k_bd, _, _, _, _ in prep])
    for prep, atts in zip(preps, all_atts):
        d_sts = [_dot_tn(vc, k_st) for _, _, _, _, vc, k_st, _ in prep]
        for (r0, q_in, _, v_bd, _, _, decay), att, d_st, tri, (_, _, s_ref, o_ref, _) in zip(
                prep, atts, d_sts, tris, dirs):
            s_t = s_ref[...]
            o = _dot(jnp.where(tri, att, 0.0).astype(BF16), v_bd) + _dot_nt(q_in, s_t.astype(BF16))
            o_ref[0, r0:r0 + CHUNK, :] = o.astype(o_ref.dtype)
            s_ref[...] = s_t * decay + jnp.where(bd, d_st, 0.0)


def _cumsum_chunks(la, tri_ref):
    hi = la.astype(BF16)
    lo = (la - hi.astype(F32)).astype(BF16)
    return _dot(tri_ref[...], hi) + _dot(tri_ref[...], lo)


def _gla_kernel(zf, laf, zb, lab, trif, trib, of_ref, ob_ref, sf_ref, sb_ref):
    @pl.when(pl.program_id(1) == 0)
    def _():
        sf_ref[...] = jnp.zeros_like(sf_ref)
        sb_ref[...] = jnp.zeros_like(sb_ref)

    _scan_both_directions(zf[0], _cumsum_chunks(laf[0], trif), zb[0], _cumsum_chunks(lab[0], trib),
                          sf_ref, sb_ref, of_ref, ob_ref)


def _ret_kernel(zf, zb, ldf, ldb, of_ref, ob_ref, sf_ref, sb_ref):
    @pl.when(pl.program_id(1) == 0)
    def _():
        sf_ref[...] = jnp.zeros_like(sf_ref)
        sb_ref[...] = jnp.zeros_like(sb_ref)

    pos = lax.broadcasted_iota(jnp.int32, (TM, QK_W), 0) % CHUNK
    bf = (pos + 1).astype(F32) * _log_sigmoid(ldf[...])
    bb = (CHUNK - pos).astype(F32) * _log_sigmoid(ldb[...])
    _scan_both_directions(zf[0], bf, zb[0], bb, sf_ref, sb_ref, of_ref, ob_ref)


def _chunk_tri(reverse):
    t = np.arange(TM)
    same = (t[:, None] // CHUNK) == (t[None, :] // CHUNK)
    order = (t[None, :] >= t[:, None]) if reverse else (t[None, :] <= t[:, None])
    return jnp.asarray(same & order, dtype=BF16)


def _bidir_scan(z, la=None, ld_f=None, ld_b=None):
    bsz, t, _ = z.shape
    nt = t // TM
    fwd = lambda w, c: pl.BlockSpec((1, TM, w), lambda b, j: (b, j, c))
    bwd = lambda w, c: pl.BlockSpec((1, TM, w), lambda b, j: (b, _bwd_tile(j, nt), c))
    full = lambda a: pl.BlockSpec(a.shape, lambda b, j: (0,) * a.ndim)
    common = dict(
        grid=(bsz, nt),
        out_specs=[fwd(V_W, 0), bwd(V_W, 0)],
        out_shape=[jax.ShapeDtypeStruct((bsz, t, V_W), BF16)] * 2,
        scratch_shapes=[pltpu.VMEM((V_W, QK_W), F32)] * 2,
        compiler_params=_params(("arbitrary", "arbitrary")),
    )
    if la is not None:
        tri_f, tri_b = _chunk_tri(False), _chunk_tri(True)
        return pl.pallas_call(
            _gla_kernel,
            in_specs=[fwd(QKV_W, 0), fwd(QK_W, 0), bwd(QKV_W, 0), bwd(QK_W, 1), full(tri_f), full(tri_b)],
            name="gla_scan", **common,
        )(z, la, z, la, tri_f, tri_b)
    return pl.pallas_call(
        _ret_kernel,
        in_specs=[fwd(QKV_W, 0), bwd(QKV_W, 0), full(ld_f), full(ld_b)],
        name="ret_scan", **common,
    )(z, z, ld_f, ld_b)


def _lru_conv(x_ref, prev_ref, next_ref, cw_ref, cb_ref, tile, nt):
    prev_ok = tile >= 2
    next_ok = jnp.logical_and(tile >= 1, tile <= nt - 2)
    prev = jnp.where(prev_ok, prev_ref[0].astype(F32), 0.0)
    nxt = jnp.where(next_ok, next_ref[0].astype(F32), 0.0)
    ext = jnp.concatenate([prev, x_ref[0].astype(F32), nxt], axis=0)
    n = ext.shape[0]
    lo, hi = PACKED_ROWS, PACKED_ROWS + TM
    cw = cw_ref[...]
    acc = ext[lo:hi] * cw[2:3]
    acc += pltpu.roll(ext, 2, axis=0)[lo:hi] * cw[0:1]
    acc += pltpu.roll(ext, 1, axis=0)[lo:hi] * cw[1:2]
    acc += pltpu.roll(ext, n - 1, axis=0)[lo:hi] * cw[3:4]
    return acc + cb_ref[...]


def _lru_one_direction(xc, w_ref, ba_ref, bx_ref, lam_ref, carry_ref, o_ref, reverse):
    half = LRU_WIDTH // 2
    xcb = xc.astype(BF16)
    pre0 = _dot(xcb[:, 0:half], w_ref[0])
    pre1 = _dot(xcb[:, half:], w_ref[1])
    pa = jnp.concatenate([pre0[:, 0:half], pre1[:, 0:half]], axis=1) + ba_ref[...]
    px = jnp.concatenate([pre0[:, half:], pre1[:, half:]], axis=1) + bx_ref[...]
    r = _sigmoid(pa)
    ig = _sigmoid(px)
    log_a = (LRU_C * r) * _log_sigmoid(lam_ref[...])
    a = jnp.exp(log_a)
    u = jnp.sqrt(1.0 - a * a) * (ig * xc)

    ng = TM // SUBLANES
    a3 = a.reshape(ng, SUBLANES, LRU_WIDTH)
    u3 = u.reshape(ng, SUBLANES, LRU_WIDTH)
    sub = lax.broadcasted_iota(jnp.int32, (ng, SUBLANES, LRU_WIDTH), 1)
    k = 1
    while k < SUBLANES:
        if reverse:
            a_s = pltpu.roll(a3, SUBLANES - k, axis=1)
            u_s = pltpu.roll(u3, SUBLANES - k, axis=1)
            ok = sub < SUBLANES - k
        else:
            a_s = pltpu.roll(a3, k, axis=1)
            u_s = pltpu.roll(u3, k, axis=1)
            ok = sub >= k
        u3 = a3 * jnp.where(ok, u_s, 0.0) + u3
        a3 = a3 * jnp.where(ok, a_s, 1.0)
        k *= 2

    edge = 0 if reverse else SUBLANES - 1
    carry = carry_ref[...]
    starts = [None] * ng
    for g in (range(ng - 1, -1, -1) if reverse else range(ng)):
        starts[g] = carry
        carry = a3[g, edge:edge + 1, :] * carry + u3[g, edge:edge + 1, :]
    carry_ref[...] = carry
    h = jnp.concatenate([a3[g] * starts[g] + u3[g] for g in range(ng)], axis=0)
    o_ref[0] = h.astype(o_ref.dtype)


def _lru_kernel(xf, xfp, xfn, xb, xbp, xbn, cw, cb,
                wf, baf, bxf, lamf, wb, bab, bxb, lamb,
                hf_ref, hb_ref, cf_ref, cbk_ref, *, nt):
    j = pl.program_id(1)

    @pl.when(j == 0)
    def _():
        cf_ref[...] = jnp.zeros_like(cf_ref)
        cbk_ref[...] = jnp.zeros_like(cbk_ref)

    xc_f = _lru_conv(xf, xfp, xfn, cw, cb, j, nt)
    _lru_one_direction(xc_f, wf, baf, bxf, lamf, cf_ref, hf_ref, False)
    xc_b = _lru_conv(xb, xbp, xbn, cw, cb, _bwd_tile(j, nt), nt)
    _lru_one_direction(xc_b, wb, bab, bxb, lamb, cbk_ref, hb_ref, True)


def _lru_scan(z, cw, cb, wf, baf, bxf, lamf, wb, bab, bxb, lamb):
    bsz, t, _ = z.shape
    w = LRU_WIDTH
    nt = t // TM
    per = TM // PACKED_ROWS
    nrow = t // PACKED_ROWS
    f_t = lambda j: j
    b_t = lambda j: _bwd_tile(j, nt)

    def specs(tile_of):
        return [
            pl.BlockSpec((1, TM, w), lambda b, j: (b, tile_of(j), EVEN_XR)),
            pl.BlockSpec((1, PACKED_ROWS, w),
                         lambda b, j: (b, jnp.maximum(tile_of(j) * per - 1, 0), EVEN_XR)),
            pl.BlockSpec((1, PACKED_ROWS, w),
                         lambda b, j: (b, jnp.minimum((tile_of(j) + 1) * per, nrow - 1), EVEN_XR)),
        ]

    full = lambda a: pl.BlockSpec(a.shape, lambda b, j: (0,) * a.ndim)
    consts = (cw, cb, wf, baf, bxf, lamf, wb, bab, bxb, lamb)
    return pl.pallas_call(
        functools.partial(_lru_kernel, nt=nt),
        grid=(bsz, nt),
        in_specs=specs(f_t) + specs(b_t) + [full(a) for a in consts],
        out_specs=[pl.BlockSpec((1, TM, w), lambda b, j: (b, j, 0)),
                   pl.BlockSpec((1, TM, w), lambda b, j: (b, b_t(j), 0))],
        out_shape=[jax.ShapeDtypeStruct((bsz, t, w), BF16)] * 2,
        scratch_shapes=[pltpu.VMEM((1, w), F32)] * 2,
        compiler_params=_params(("arbitrary", "arbitrary")),
        name="lru_scan",
    )(z, z, z, z, z, z, *consts)


N_KEYS = 3 * BLOCK + CTX_LEN


def _attn_kernel(q_ref, bias_ref, sink_ref, kp_ref, kc_ref, kn_ref, kx_ref, o_ref):
    kv = jnp.concatenate([kp_ref[0], kc_ref[0], kn_ref[0], kx_ref[0]], axis=0)
    kk, kr, vv, vr = (kv[:, i * LANES:(i + 1) * LANES] for i in range(4))
    low = lax.broadcasted_iota(jnp.int32, (2 * BLOCK, LANES), 1) < HEAD_DIM
    keep_lo = jnp.where(low, 1.0, 0.0).astype(BF16)
    keep_hi = jnp.where(low, 0.0, 1.0).astype(BF16)
    bias = bias_ref[0]
    bias2 = jnp.concatenate([bias, bias], axis=0)
    q = q_ref[0]
    group = (SWA_QH // SWA_KVH) * HEAD_DIM

    scores = []
    for h in range(SWA_KVH):
        q2 = jnp.concatenate([q[:, h * group:h * group + LANES],
                              q[:, h * group + LANES:(h + 1) * group]], axis=0)
        k_lo, k_hi = (kk, kr) if h == 0 else (kr, kk)
        scores.append(_dot_nt(q2 * keep_lo, k_lo) + bias2)
        scores.append(_dot_nt(q2 * keep_hi, k_hi) + bias2)

    sinks = [sink_ref[h, par][:, 0:1] for h in range(SWA_KVH) for par in range(2)]
    maxes = [jnp.maximum(jnp.max(s, axis=-1, keepdims=True), sk) for s, sk in zip(scores, sinks)]
    probs = [jnp.exp(s - m) for s, m in zip(scores, maxes)]
    dens = [jnp.sum(p, axis=-1, keepdims=True) + jnp.exp(sk - m) for p, sk, m in zip(probs, sinks, maxes)]
    vals = [vv, vr, vr, vv]
    outs = [_dot(p.astype(BF16), v_sel) / den for p, v_sel, den in zip(probs, vals, dens)]

    for h in range(SWA_KVH):
        o2 = jnp.where(low, outs[2 * h], outs[2 * h + 1]).astype(o_ref.dtype)
        o_ref[0, :, h * group:h * group + LANES] = o2[0:BLOCK]
        o_ref[0, :, h * group + LANES:(h + 1) * group] = o2[BLOCK:2 * BLOCK]


def _attn_bias():
    qi = np.arange(BLOCK)[:, None]
    kj = np.arange(3 * BLOCK)[None, :]
    near = np.abs(kj - BLOCK - qi) <= WINDOW
    blk = kj // BLOCK
    out = np.zeros((5, BLOCK, N_KEYS), np.float32)
    for var in range(4):
        ok = near & ((blk != 0) | bool(var & 2)) & ((blk != 2) | bool(var & 1))
        out[var, :, :3 * BLOCK] = np.where(ok, 0.0, NEG_INF)
    out[4, :, :3 * BLOCK] = NEG_INF
    return jnp.asarray(out)


def _attention(z, sink_tab):
    bsz, t, _ = z.shape
    qw = SWA_QH * HEAD_DIM
    nb = t // BLOCK
    cblk = CTX_LEN // BLOCK
    at = lambda off: pl.BlockSpec((1, BLOCK, 4 * LANES), lambda b, u: (b, jnp.clip(u + off, 0, nb - 1), ODD_KV))
    ctx = pl.BlockSpec((1, CTX_LEN, 4 * LANES), lambda b, u: (b, 0, ODD_KV))

    def variant(b, u):
        prev_ok = (u - 1 >= cblk).astype(jnp.int32)
        next_ok = (u + 1 <= nb - 1).astype(jnp.int32)
        return (jnp.where(u < cblk, 4, 2 * prev_ok + next_ok), 0, 0)

    return pl.pallas_call(
        _attn_kernel,
        grid=(bsz, nb),
        in_specs=[pl.BlockSpec((1, BLOCK, qw), lambda b, u: (b, u, ODD_Q)),
                  pl.BlockSpec((1, BLOCK, N_KEYS), variant),
                  pl.BlockSpec(sink_tab.shape, lambda b, u: (0, 0, 0, 0)),
                  at(-1), at(0), at(1), ctx],
        out_specs=pl.BlockSpec((1, BLOCK, qw), lambda b, u: (b, u, 0)),
        out_shape=jax.ShapeDtypeStruct((bsz, t, qw), BF16),
        compiler_params=_params(("arbitrary", "arbitrary")),
        name="swa_attn",
    )(z, _attn_bias(), sink_tab, z, z, z, z)


def _head_norm(o, g_row):
    outs = []
    for hd in range(o.shape[1] // LANES):
        oh = o[:, hd * LANES:(hd + 1) * LANES]
        ms = jnp.mean(oh * oh, axis=-1, keepdims=True)
        outs.append(oh * lax.rsqrt(ms + EPS) * g_row)
    return jnp.concatenate(outs, axis=1)


def _finish(x, gate, y, w_ref, fg_ref, final):
    xn = x + gate * _dot(y.astype(BF16), w_ref[...])
    if final:
        ms = jnp.mean(xn * xn, axis=-1, keepdims=True)
        xn = xn * lax.rsqrt(ms + EPS) * fg_ref[...]
    return xn


def _out_even_kernel(x_ref, mod_ref, of_ref, ob_ref, gg_ref, hf_ref, hb_ref, gl_ref, ng_ref, w_ref, fg_ref,
                     o_ref, *, final, pieces):
    d = D_MODEL
    for sub in range(pieces):
        rows = slice(sub * TM, (sub + 1) * TM)
        ld = lambda r: r[0, rows, :].astype(F32)
        gla = _head_norm(ld(of_ref) + ld(ob_ref), ng_ref[...]) * _silu(ld(gg_ref))
        lru = (ld(hf_ref) + ld(hb_ref)) * _silu(ld(gl_ref))
        gate = _mod_row(mod_ref, sub, final)[:, 2 * d:3 * d]
        o_ref[0, rows, :] = _finish(x_ref[0, rows, :], gate, jnp.concatenate([gla, lru], axis=1),
                                    w_ref, fg_ref, final)


def _out_odd_kernel(x_ref, mod_ref, att_ref, gs_ref, of_ref, ob_ref, gr_ref, ng_ref, w_ref, fg_ref,
                    o_ref, *, final, pieces):
    d = D_MODEL
    for sub in range(pieces):
        rows = slice(sub * TM, (sub + 1) * TM)
        ld = lambda r: r[0, rows, :].astype(F32)
        att = ld(att_ref) * _silu(ld(gs_ref))
        ret = _head_norm(ld(of_ref) + ld(ob_ref), ng_ref[...]) * _silu(ld(gr_ref))
        gate = _mod_row(mod_ref, sub, final)[:, 2 * d:3 * d]
        o_ref[0, rows, :] = _finish(x_ref[0, rows, :], gate, jnp.concatenate([att, ret], axis=1),
                                    w_ref, fg_ref, final)


def _out_proj(body, xs, mod, tok_inputs, ng, w_out, fg, final):
    bsz, t, d = xs.shape
    tp = TM if final else _proj_tile(t)
    skip = 1 if final else 0
    tok = lambda w, c: pl.BlockSpec((1, tp, w), lambda b, j: (b, j + skip, c))
    full = lambda a: pl.BlockSpec(a.shape, lambda b, j: (0,) * a.ndim)
    return pl.pallas_call(
        functools.partial(body, final=final, pieces=tp // TM),
        grid=(bsz, t // tp - skip),
        in_specs=[tok(d, 0), full(mod)] + [tok(V_W, c) for _, c in tok_inputs]
                 + [full(ng), full(w_out), full(fg)],
        out_specs=pl.BlockSpec((1, tp, d), lambda b, j: (b, j, 0)),
        out_shape=jax.ShapeDtypeStruct((bsz, t - skip * TM, d), F32),
        compiler_params=_params(("arbitrary", "arbitrary")),
        name="out_proj",
    )(xs, mod, *[a for a, _ in tok_inputs], ng, w_out, fg)


def _block_diag_gates(wa, wx):
    lb = wa.shape[-1]
    z = jnp.zeros((lb, lb), wa.dtype)

    def pair(w, i):
        return jnp.concatenate([jnp.concatenate([w[2 * i], z], axis=1),
                                jnp.concatenate([z, w[2 * i + 1]], axis=1)], axis=0)

    return jnp.stack([jnp.concatenate([pair(wa, i), pair(wx, i)], axis=1) for i in range(2)]).astype(BF16)


def _rope_tables(n_tok):
    rows = n_tok // GRID_W
    row = jnp.repeat(jnp.arange(rows), GRID_W)
    col = jnp.tile(jnp.arange(GRID_W), rows)
    n_freq = HEAD_DIM // 4
    inv = ROPE_BASE ** (-jnp.arange(n_freq, dtype=F32) / n_freq)
    ang = jnp.concatenate([row.astype(F32)[:, None] * inv[None],
                           col.astype(F32)[:, None] * inv[None]], axis=-1)
    cos, sin = jnp.cos(ang), jnp.sin(ang)
    reps = LANES // HEAD_DIM
    cos_f = jnp.tile(cos, (1, 2 * reps))
    sin_s = jnp.tile(jnp.concatenate([-sin, sin], axis=-1), (1, reps))
    cos_f = jnp.concatenate([jnp.ones((CTX_LEN, LANES), F32), cos_f], axis=0)
    sin_s = jnp.concatenate([jnp.zeros((CTX_LEN, LANES), F32), sin_s], axis=0)
    return cos_f, sin_s


def kernel(x, c, ctx, c_ctx, ada_w, ada_b, norm_g, e_w_in, gla_up_fw, gla_b_fw, gla_up_bw, gla_b_bw, gla_norm_g, lru_conv_w, lru_conv_b, lru_wa_fw, lru_ba_fw, lru_wx_fw, lru_bx_fw, lru_lam_fw, lru_wa_bw, lru_ba_bw, lru_wx_bw, lru_bx_bw, lru_lam_bw, e_w_out, o_w_in, swa_sink, ret_dec_fw, ret_dec_bw, ret_norm_g, o_w_out, final_g):
    bsz, n_tok, d = x.shape
    depth = ada_w.shape[0]
    assert d == D_MODEL and ctx.shape[1] == CTX_LEN and n_tok % TM == 0 and bsz <= CTX_ROW

    cvec = jnp.zeros((SUBLANES, d), F32).at[0:bsz].set(c).at[CTX_ROW].set(c_ctx)
    mods = _ada_mod(cvec, ada_w, ada_b)
    xs = jnp.concatenate([ctx, x], axis=1)
    cos_f, sin_s = _rope_tables(n_tok)
    row = lambda a: a.reshape(1, -1)
    fg = row(final_g)

    for i in range(depth):
        jj = i // 2
        final = i == depth - 1
        mod = mods[i]
        g = row(norm_g[i])
        if i % 2 == 0:
            w = e_w_in[jj]
            w_all = jnp.concatenate(
                [w[:, 0:1536], w[:, 1568:2592], w[:, 1536:1568], jnp.zeros((d, LANES - 2 * GLA_RANK), F32)],
                axis=1).astype(BF16)
            up_pad = jnp.zeros((LANES, 2 * QK_W), F32)
            up_pad = up_pad.at[0:GLA_RANK, 0:QK_W].set(gla_up_fw[jj])
            up_pad = up_pad.at[GLA_RANK:2 * GLA_RANK, QK_W:].set(gla_up_bw[jj]).astype(BF16)
            ub = jnp.concatenate([gla_b_fw[jj], gla_b_bw[jj]]).reshape(1, -1)
            z, la = _proj_even(xs, mod, g, w_all, up_pad, ub)
            o_f, o_b = _bidir_scan(z, la=la)
            h_f, h_b = _lru_scan(
                z, lru_conv_w[jj], row(lru_conv_b[jj]),
                _block_diag_gates(lru_wa_fw[jj], lru_wx_fw[jj]), row(lru_ba_fw[jj]), row(lru_bx_fw[jj]),
                row(lru_lam_fw[jj]),
                _block_diag_gates(lru_wa_bw[jj], lru_wx_bw[jj]), row(lru_ba_bw[jj]), row(lru_bx_bw[jj]),
                row(lru_lam_bw[jj]))
            xs = _out_proj(_out_even_kernel, xs, mod,
                           ((o_f, 0), (o_b, 0), (z, EVEN_GG), (h_f, 0), (h_b, 0), (z, EVEN_GL)),
                           row(gla_norm_g[jj]), e_w_out[jj].astype(BF16), fg, final)
        else:
            z = _proj_odd(xs, mod, g, o_w_in[jj].astype(BF16), cos_f, sin_s)
            sink = swa_sink[jj].reshape(SWA_KVH, 2, 2)
            sink_tab = jnp.broadcast_to(
                jnp.transpose(sink, (0, 2, 1))[:, :, :, None, None],
                (SWA_KVH, 2, 2, BLOCK, LANES)).reshape(SWA_KVH, 2, 2 * BLOCK, LANES)
            att = _attention(z, sink_tab)
            ld_f = row(jnp.repeat(ret_dec_fw[jj], RET_DK))
            ld_b = row(jnp.repeat(ret_dec_bw[jj], RET_DK))
            o_f, o_b = _bidir_scan(z, ld_f=ld_f, ld_b=ld_b)
            xs = _out_proj(_out_odd_kernel, xs, mod,
                           ((att, 0), (z, ODD_GS), (o_f, 0), (o_b, 0), (z, ODD_GR)),
                           row(ret_norm_g[jj]), o_w_out[jj].astype(BF16), fg, final)
    return xs
```

```python
import functools

import numpy as np
import jax
import jax.numpy as jnp
from jax import lax
from jax.experimental import pallas as pl
from jax.experimental.pallas import tpu as pltpu

F32 = jnp.float32
BF16 = jnp.bfloat16

D_MODEL = 1024
CTX_LEN = 256
GRID_W = 64
EPS = 1e-6
NEG_INF = -1e30
ROPE_BASE = 10000.0
CHUNK = 64

GLA_HEADS = 4
GLA_DK = 64
GLA_DV = 128
GLA_RANK = 16
GLA_TEMP = 16.0
LRU_WIDTH = 512
LRU_BLOCKS = 4
LRU_CONV = 4
LRU_C = 8.0
HEAD_DIM = 64
SWA_QH = 8
SWA_KVH = 2
WINDOW = 128
BLOCK = 128
RET_HEADS = 4
RET_DK = 64
RET_DV = 128

LANES = 128
SUBLANES = 8
PACKED_ROWS = 16
TM = 256
CTX_ROW = SUBLANES - 1
QK_W = GLA_HEADS * GLA_DK
V_W = GLA_HEADS * GLA_DV
QKV_W = 2 * QK_W + V_W
VMEM_LIMIT = 56 * 1024 * 1024

EVEN_GG, EVEN_XR, EVEN_GL, EVEN_COLS = 2, 3, 4, 2560
ODD_Q, ODD_KV, ODD_GS, ODD_GR, ODD_COLS = 2, 3, 4, 5, 3072


def _params(sem):
    return pltpu.CompilerParams(dimension_semantics=sem, vmem_limit_bytes=VMEM_LIMIT)


def _dot(a, b):
    return jnp.dot(a, b, preferred_element_type=F32)


def _dot_nt(a, b):
    return lax.dot_general(a, b, (((1,), (1,)), ((), ())), preferred_element_type=F32)


def _dot_tn(a, b):
    return lax.dot_general(a, b, (((0,), (0,)), ((), ())), preferred_element_type=F32)


def _sigmoid(x):
    return 1.0 / (1.0 + jnp.exp(-x))


def _silu(x):
    return x * _sigmoid(x)


def _log_sigmoid(x):
    return jnp.minimum(x, 0.0) - jnp.log(1.0 + jnp.exp(-jnp.abs(x)))


def _proj_tile(t):
    n = t // TM
    for k in (5, 4, 3, 2, 1):
        if n % k == 0:
            return k * TM
    return TM


def _ada_kernel(c_ref, w_ref, b_ref, o_ref):
    s = _silu(c_ref[...])
    o_ref[0] = _dot(s.astype(BF16), w_ref[0].astype(BF16)) + b_ref[0]


def _ada_mod(cvec, ada_w, ada_b):
    depth, d, d3 = ada_w.shape
    nsplit = d3 // d
    return pl.pallas_call(
        _ada_kernel,
        grid=(depth, nsplit),
        in_specs=[
            pl.BlockSpec((SUBLANES, d), lambda i, n: (0, 0)),
            pl.BlockSpec((1, d, d), lambda i, n: (i, 0, n)),
            pl.BlockSpec((1, 1, d), lambda i, n: (i, 0, n)),
        ],
        out_specs=pl.BlockSpec((1, SUBLANES, d), lambda i, n: (i, 0, n)),
        out_shape=jax.ShapeDtypeStruct((depth, SUBLANES, d3), F32),
        compiler_params=_params(("arbitrary", "arbitrary")),
        name="ada_mod",
    )(cvec, ada_w, ada_b.reshape(depth, 1, d3))


def _mod_row(mod_ref, sub, skip):
    if skip:
        return mod_ref[pl.ds(pl.program_id(0), 1), :]
    is_ctx = jnp.logical_and(pl.program_id(1) == 0, sub == 0)
    return mod_ref[pl.ds(jnp.where(is_ctx, CTX_ROW, pl.program_id(0)), 1), :]


def _modulated(x, m, g_row):
    d = D_MODEL
    ms = jnp.mean(x * x, axis=-1, keepdims=True)
    return (x * lax.rsqrt(ms + EPS)) * g_row * (1.0 + m[:, d:2 * d]) + m[:, 0:d]


def _proj_even_kernel(x_ref, mod_ref, g_ref, w_ref, up_ref, ub_ref, z_ref, la_ref, *, pieces):
    for sub in range(pieces):
        rows = slice(sub * TM, (sub + 1) * TM)
        h = _modulated(x_ref[0, rows, :], _mod_row(mod_ref, sub, False), g_ref[...])
        z = _dot(h.astype(BF16), w_ref[...])
        z_ref[0, rows, 0:QK_W] = (z[:, 0:QK_W] * (GLA_DK ** -0.5)).astype(BF16)
        z_ref[0, rows, QK_W:EVEN_COLS] = z[:, QK_W:EVEN_COLS].astype(BF16)
        pre = _dot(z[:, EVEN_COLS:EVEN_COLS + LANES].astype(BF16), up_ref[...]) + ub_ref[...]
        la_ref[0, rows, :] = _log_sigmoid(pre) * (1.0 / GLA_TEMP)


def _proj_even(xs, mod, g, w_all, up_pad, ub):
    bsz, t, d = xs.shape
    tp = _proj_tile(t)
    tok = lambda w: pl.BlockSpec((1, tp, w), lambda b, j: (b, j, 0))
    full = lambda a: pl.BlockSpec(a.shape, lambda b, j: (0,) * a.ndim)
    return pl.pallas_call(
        functools.partial(_proj_even_kernel, pieces=tp // TM),
        grid=(bsz, t // tp),
        in_specs=[tok(d), full(mod), full(g), full(w_all), full(up_pad), full(ub)],
        out_specs=[tok(EVEN_COLS), tok(2 * QK_W)],
        out_shape=[jax.ShapeDtypeStruct((bsz, t, EVEN_COLS), BF16),
                   jax.ShapeDtypeStruct((bsz, t, 2 * QK_W), F32)],
        compiler_params=_params(("arbitrary", "arbitrary")),
        name="proj_even",
    )(xs, mod, g, w_all, up_pad, ub)


def _rope(x, cos_f, sin_s, lane):
    outs = []
    for g in range(x.shape[1] // LANES):
        xg = x[:, g * LANES:(g + 1) * LANES]
        up = pltpu.roll(xg, LANES - HEAD_DIM // 2, axis=1)
        dn = pltpu.roll(xg, HEAD_DIM // 2, axis=1)
        swapped = jnp.where(lane % HEAD_DIM < HEAD_DIM // 2, up, dn)
        outs.append(xg * cos_f + swapped * sin_s)
    return outs[0] if len(outs) == 1 else jnp.concatenate(outs, axis=1)


def _proj_odd_kernel(x_ref, mod_ref, g_ref, w_ref, cos_ref, sin_ref, z_ref, *, pieces):
    lane = lax.broadcasted_iota(jnp.int32, (TM, LANES), 1)
    for sub in range(pieces):
        rows = slice(sub * TM, (sub + 1) * TM)
        h = _modulated(x_ref[0, rows, :], _mod_row(mod_ref, sub, False), g_ref[...])
        z = _dot(h.astype(BF16), w_ref[...])
        cos_f = cos_ref[rows, :]
        sin_s = sin_ref[rows, :]
        rope = lambda a: _rope(a, cos_f, sin_s, lane)
        k = rope(z[:, 512:640])
        v = z[:, 640:768]
        c0 = ODD_Q * 512
        z_ref[0, rows, 0:QK_W] = rope(z[:, 1280:1536]).astype(BF16)
        z_ref[0, rows, QK_W:2 * QK_W] = (rope(z[:, 1536:1792]) * (RET_DK ** -0.5)).astype(BF16)
        z_ref[0, rows, 2 * QK_W:QKV_W] = z[:, 1792:2304].astype(BF16)
        z_ref[0, rows, c0:c0 + 512] = (rope(z[:, 0:512]) * (HEAD_DIM ** -0.5)).astype(BF16)
        z_ref[0, rows, c0 + 512:c0 + 640] = k.astype(BF16)
        z_ref[0, rows, c0 + 640:c0 + 768] = pltpu.roll(k, HEAD_DIM, axis=1).astype(BF16)
        z_ref[0, rows, c0 + 768:c0 + 896] = v.astype(BF16)
        z_ref[0, rows, c0 + 896:c0 + 1024] = pltpu.roll(v, HEAD_DIM, axis=1).astype(BF16)
        z_ref[0, rows, ODD_GS * 512:(ODD_GS + 1) * 512] = z[:, 768:1280].astype(BF16)
        z_ref[0, rows, ODD_GR * 512:(ODD_GR + 1) * 512] = z[:, 2304:2816].astype(BF16)


def _proj_odd(xs, mod, g, w_in, cos_f, sin_s):
    bsz, t, d = xs.shape
    tp = _proj_tile(t)
    tok = lambda w: pl.BlockSpec((1, tp, w), lambda b, j: (b, j, 0))
    full = lambda a: pl.BlockSpec(a.shape, lambda b, j: (0,) * a.ndim)
    tab = pl.BlockSpec((tp, LANES), lambda b, j: (j, 0))
    return pl.pallas_call(
        functools.partial(_proj_odd_kernel, pieces=tp // TM),
        grid=(bsz, t // tp),
        in_specs=[tok(d), full(mod), full(g), full(w_in), tab, tab],
        out_specs=tok(ODD_COLS),
        out_shape=jax.ShapeDtypeStruct((bsz, t, ODD_COLS), BF16),
        compiler_params=_params(("arbitrary", "arbitrary")),
        name="proj_odd",
    )(xs, mod, g, w_in, cos_f, sin_s)


def _bwd_tile(j, nt):
    return jnp.where(j == 0, 0, nt - j)


def _scan_both_directions(zf, bcum_f, zb, bcum_b, sf_ref, sb_ref, of_ref, ob_ref):
    nch = TM // CHUNK
    col_k = lax.broadcasted_iota(jnp.int32, (CHUNK, QK_W), 1)
    row_k = lax.broadcasted_iota(jnp.int32, (CHUNK, QK_W), 0)
    col_v = lax.broadcasted_iota(jnp.int32, (CHUNK, V_W), 1)
    tris = ((col_k % CHUNK) <= row_k, (col_k % CHUNK) >= row_k)
    bd = (lax.broadcasted_iota(jnp.int32, (V_W, QK_W), 0) // GLA_DV
          == lax.broadcasted_iota(jnp.int32, (V_W, QK_W), 1) // GLA_DK)
    dirs = ((zf, bcum_f, sf_ref, of_ref, False), (zb, bcum_b, sb_ref, ob_ref, True))
    preps, all_atts = [], []
    for step in range(nch):
        prep = []
        for qkv, bcum, _, _, reverse in dirs:
            r0 = (nch - 1 - step if reverse else step) * CHUNK
            b = bcum[r0:r0 + CHUNK]
            last = b[0:1] if reverse else b[CHUNK - 1:CHUNK]
            qc = qkv[r0:r0 + CHUNK, 0:QK_W].astype(F32)
            kc = qkv[r0:r0 + CHUNK, QK_W:2 * QK_W].astype(F32)
            vc = qkv[r0:r0 + CHUNK, 2 * QK_W:QKV_W]
            q_in = (qc * jnp.exp(b)).astype(BF16)
            k_in = kc * jnp.exp(-b)
            k_st = (kc * jnp.exp(last - b)).astype(BF16)
            k_bd = jnp.concatenate(
                [jnp.where(col_k // GLA_DK == h, k_in, 0.0) for h in range(GLA_HEADS)], axis=0).astype(BF16)
            v_bd = jnp.concatenate(
                [jnp.where(col_v // GLA_DV == h, vc, jnp.zeros_like(vc)) for h in range(GLA_HEADS)], axis=0)
            prep.append((r0, q_in, k_bd, v_bd, vc, k_st, jnp.exp(last)))
        preps.append(prep)
        all_atts.append([_dot_nt(q_in, k_bd) for _, q_in, k_bd, _, _, _, _ in prep])
    for prep, atts in zip(preps, all_atts):
        d_sts = [_dot_tn(vc, k_st) for _, _, _, _, vc, k_st, _ in prep]
        for (r0, q_in, _, v_bd, _, _, decay), att, d_st, tri, (_, _, s_ref, o_ref, _) in zip(
                prep, atts, d_sts, tris, dirs):
            s_t = s_ref[...]
            o = _dot(jnp.where(tri, att, 0.0).astype(BF16), v_bd) + _dot_nt(q_in, s_t.astype(BF16))
            o_ref[0, r0:r0 + CHUNK, :] = o.astype(o_ref.dtype)
            s_ref[...] = s_t * decay + jnp.where(bd, d_st, 0.0)


def _cumsum_chunks(la, tri_ref):
    hi = la.astype(BF16)
    lo = (la - hi.astype(F32)).astype(BF16)
    return _dot(tri_ref[...], hi) + _dot(tri_ref[...], lo)


def _gla_kernel(zf, laf, zb, lab, trif, trib, of_ref, ob_ref, sf_ref, sb_ref):
    @pl.when(pl.program_id(1) == 0)
    def _():
        sf_ref[...] = jnp.zeros_like(sf_ref)
        sb_ref[...] = jnp.zeros_like(sb_ref)

    _scan_both_directions(zf[0], _cumsum_chunks(laf[0], trif), zb[0], _cumsum_chunks(lab[0], trib),
                          sf_ref, sb_ref, of_ref, ob_ref)


def _ret_kernel(zf, zb, ldf, ldb, of_ref, ob_ref, sf_ref, sb_ref):
    @pl.when(pl.program_id(1) == 0)
    def _():
        sf_ref[...] = jnp.zeros_like(sf_ref)
        sb_ref[...] = jnp.zeros_like(sb_ref)

    pos = lax.broadcasted_iota(jnp.int32, (TM, QK_W), 0) % CHUNK
    bf = (pos + 1).astype(F32) * _log_sigmoid(ldf[...])
    bb = (CHUNK - pos).astype(F32) * _log_sigmoid(ldb[...])
    _scan_both_directions(zf[0], bf, zb[0], bb, sf_ref, sb_ref, of_ref, ob_ref)


def _chunk_tri(reverse):
    t = np.arange(TM)
    same = (t[:, None] // CHUNK) == (t[None, :] // CHUNK)
    order = (t[None, :] >= t[:, None]) if reverse else (t[None, :] <= t[:, None])
    return jnp.asarray(same & order, dtype=BF16)


def _bidir_scan(z, la=None, ld_f=None, ld_b=None):
    bsz, t, _ = z.shape
    nt = t // TM
    fwd = lambda w, c: pl.BlockSpec((1, TM, w), lambda b, j: (b, j, c))
    bwd = lambda w, c: pl.BlockSpec((1, TM, w), lambda b, j: (b, _bwd_tile(j, nt), c))
    full = lambda a: pl.BlockSpec(a.shape, lambda b, j: (0,) * a.ndim)
    common = dict(
        grid=(bsz, nt),
        out_specs=[fwd(V_W, 0), bwd(V_W, 0)],
        out_shape=[jax.ShapeDtypeStruct((bsz, t, V_W), BF16)] * 2,
        scratch_shapes=[pltpu.VMEM((V_W, QK_W), F32)] * 2,
        compiler_params=_params(("arbitrary", "arbitrary")),
    )
    if la is not None:
        tri_f, tri_b = _chunk_tri(False), _chunk_tri(True)
        return pl.pallas_call(
            _gla_kernel,
            in_specs=[fwd(QKV_W, 0), fwd(QK_W, 0), bwd(QKV_W, 0), bwd(QK_W, 1), full(tri_f), full(tri_b)],
            name="gla_scan", **common,
        )(z, la, z, la, tri_f, tri_b)
    return pl.pallas_call(
        _ret_kernel,
        in_specs=[fwd(QKV_W, 0), bwd(QKV_W, 0), full(ld_f), full(ld_b)],
        name="ret_scan", **common,
    )(z, z, ld_f, ld_b)


def _lru_conv(x_ref, prev_ref, next_ref, cw_ref, cb_ref, tile, nt):
    prev_ok = tile >= 2
    next_ok = jnp.logical_and(tile >= 1, tile <= nt - 2)
    prev = jnp.where(prev_ok, prev_ref[0].astype(F32), 0.0)
    nxt = jnp.where(next_ok, next_ref[0].astype(F32), 0.0)
    ext = jnp.concatenate([prev, x_ref[0].astype(F32), nxt], axis=0)
    n = ext.shape[0]
    lo, hi = PACKED_ROWS, PACKED_ROWS + TM
    cw = cw_ref[...]
    acc = ext[lo:hi] * cw[2:3]
    acc += pltpu.roll(ext, 2, axis=0)[lo:hi] * cw[0:1]
    acc += pltpu.roll(ext, 1, axis=0)[lo:hi] * cw[1:2]
    acc += pltpu.roll(ext, n - 1, axis=0)[lo:hi] * cw[3:4]
    return acc + cb_ref[...]


def _lru_one_direction(xc, w_ref, ba_ref, bx_ref, lam_ref, carry_ref, o_ref, reverse):
    half = LRU_WIDTH // 2
    xcb = xc.astype(BF16)
    pre0 = _dot(xcb[:, 0:half], w_ref[0])
    pre1 = _dot(xcb[:, half:], w_ref[1])
    pa = jnp.concatenate([pre0[:, 0:half], pre1[:, 0:half]], axis=1) + ba_ref[...]
    px = jnp.concatenate([pre0[:, half:], pre1[:, half:]], axis=1) + bx_ref[...]
    r = _sigmoid(pa)
    ig = _sigmoid(px)
    log_a = (LRU_C * r) * _log_sigmoid(lam_ref[...])
    a = jnp.exp(log_a)
    u = jnp.sqrt(1.0 - a * a) * (ig * xc)

    ng = TM // SUBLANES
    a3 = a.reshape(ng, SUBLANES, LRU_WIDTH)
    u3 = u.reshape(ng, SUBLANES, LRU_WIDTH)
    sub = lax.broadcasted_iota(jnp.int32, (ng, SUBLANES, LRU_WIDTH), 1)
    k = 1
    while k < SUBLANES:
        if reverse:
            a_s = pltpu.roll(a3, SUBLANES - k, axis=1)
            u_s = pltpu.roll(u3, SUBLANES - k, axis=1)
            ok = sub < SUBLANES - k
        else:
            a_s = pltpu.roll(a3, k, axis=1)
            u_s = pltpu.roll(u3, k, axis=1)
            ok = sub >= k
        u3 = a3 * jnp.where(ok, u_s, 0.0) + u3
        a3 = a3 * jnp.where(ok, a_s, 1.0)
        k *= 2

    edge = 0 if reverse else SUBLANES - 1
    carry = carry_ref[...]
    starts = [None] * ng
    for g in (range(ng - 1, -1, -1) if reverse else range(ng)):
        starts[g] = carry
        carry = a3[g, edge:edge + 1, :] * carry + u3[g, edge:edge + 1, :]
    carry_ref[...] = carry
    h = jnp.concatenate([a3[g] * starts[g] + u3[g] for g in range(ng)], axis=0)
    o_ref[0] = h.astype(o_ref.dtype)


def _lru_kernel(xf, xfp, xfn, xb, xbp, xbn, cw, cb,
                wf, baf, bxf, lamf, wb, bab, bxb, lamb,
                hf_ref, hb_ref, cf_ref, cbk_ref, *, nt):
    j = pl.program_id(1)

    @pl.when(j == 0)
    def _():
        cf_ref[...] = jnp.zeros_like(cf_ref)
        cbk_ref[...] = jnp.zeros_like(cbk_ref)

    xc_f = _lru_conv(xf, xfp, xfn, cw, cb, j, nt)
    _lru_one_direction(xc_f, wf, baf, bxf, lamf, cf_ref, hf_ref, False)
    xc_b = _lru_conv(xb, xbp, xbn, cw, cb, _bwd_tile(j, nt), nt)
    _lru_one_direction(xc_b, wb, bab, bxb, lamb, cbk_ref, hb_ref, True)


def _lru_scan(z, cw, cb, wf, baf, bxf, lamf, wb, bab, bxb, lamb):
    bsz, t, _ = z.shape
    w = LRU_WIDTH
    nt = t // TM
    per = TM // PACKED_ROWS
    nrow = t // PACKED_ROWS
    f_t = lambda j: j
    b_t = lambda j: _bwd_tile(j, nt)

    def specs(tile_of):
        return [
            pl.BlockSpec((1, TM, w), lambda b, j: (b, tile_of(j), EVEN_XR)),
            pl.BlockSpec((1, PACKED_ROWS, w),
                         lambda b, j: (b, jnp.maximum(tile_of(j) * per - 1, 0), EVEN_XR)),
            pl.BlockSpec((1, PACKED_ROWS, w),
                         lambda b, j: (b, jnp.minimum((tile_of(j) + 1) * per, nrow - 1), EVEN_XR)),
        ]

    full = lambda a: pl.BlockSpec(a.shape, lambda b, j: (0,) * a.ndim)
    consts = (cw, cb, wf, baf, bxf, lamf, wb, bab, bxb, lamb)
    return pl.pallas_call(
        functools.partial(_lru_kernel, nt=nt),
        grid=(bsz, nt),
        in_specs=specs(f_t) + specs(b_t) + [full(a) for a in consts],
        out_specs=[pl.BlockSpec((1, TM, w), lambda b, j: (b, j, 0)),
                   pl.BlockSpec((1, TM, w), lambda b, j: (b, b_t(j), 0))],
        out_shape=[jax.ShapeDtypeStruct((bsz, t, w), BF16)] * 2,
        scratch_shapes=[pltpu.VMEM((1, w), F32)] * 2,
        compiler_params=_params(("arbitrary", "arbitrary")),
        name="lru_scan",
    )(z, z, z, z, z, z, *consts)


N_KEYS = 3 * BLOCK + CTX_LEN
ATTN_QB = 2


def _attn_kernel(q_ref, b0_ref, b1_ref, sink_ref, k0_ref, k1_ref, k2_ref, k3_ref, kx_ref, o_ref):
    blocks = [r[0] for r in (k0_ref, k1_ref, k2_ref, k3_ref)]
    ctx = kx_ref[0]
    low = lax.broadcasted_iota(jnp.int32, (2 * BLOCK, LANES), 1) < HEAD_DIM
    keep_lo = jnp.where(low, 1.0, 0.0).astype(BF16)
    keep_hi = jnp.where(low, 0.0, 1.0).astype(BF16)
    group = (SWA_QH // SWA_KVH) * HEAD_DIM
    sinks = [sink_ref[h, par][:, 0:1] for h in range(SWA_KVH) for par in range(2)] * ATTN_QB

    scores, vals = [], []
    for i, bias_ref in enumerate((b0_ref, b1_ref)):
        kv = jnp.concatenate(blocks[i:i + 3] + [ctx], axis=0)
        kk, kr, vv, vr = (kv[:, n * LANES:(n + 1) * LANES] for n in range(4))
        bias = bias_ref[0]
        bias2 = jnp.concatenate([bias, bias], axis=0)
        q = q_ref[0, i * BLOCK:(i + 1) * BLOCK, :]
        for h in range(SWA_KVH):
            q2 = jnp.concatenate([q[:, h * group:h * group + LANES],
                                  q[:, h * group + LANES:(h + 1) * group]], axis=0)
            k_lo, k_hi = (kk, kr) if h == 0 else (kr, kk)
            scores.append(_dot_nt(q2 * keep_lo, k_lo) + bias2)
            scores.append(_dot_nt(q2 * keep_hi, k_hi) + bias2)
        vals += [vv, vr, vr, vv]

    maxes = [jnp.maximum(jnp.max(s, axis=-1, keepdims=True), sk) for s, sk in zip(scores, sinks)]
    probs = [jnp.exp(s - m) for s, m in zip(scores, maxes)]
    dens = [jnp.sum(p, axis=-1, keepdims=True) + jnp.exp(sk - m) for p, sk, m in zip(probs, sinks, maxes)]
    outs = [_dot(p.astype(BF16), v_sel) / den for p, v_sel, den in zip(probs, vals, dens)]

    for i in range(ATTN_QB):
        for h in range(SWA_KVH):
            n = 2 * (SWA_KVH * i + h)
            o2 = jnp.where(low, outs[n], outs[n + 1]).astype(o_ref.dtype)
            o_ref[0, i * BLOCK:(i + 1) * BLOCK, h * group:h * group + LANES] = o2[0:BLOCK]
            o_ref[0, i * BLOCK:(i + 1) * BLOCK, h * group + LANES:(h + 1) * group] = o2[BLOCK:2 * BLOCK]


def _attn_bias():
    qi = np.arange(BLOCK)[:, None]
    kj = np.arange(3 * BLOCK)[None, :]
    near = np.abs(kj - BLOCK - qi) <= WINDOW
    blk = kj // BLOCK
    out = np.zeros((5, BLOCK, N_KEYS), np.float32)
    for var in range(4):
        ok = near & ((blk != 0) | bool(var & 2)) & ((blk != 2) | bool(var & 1))
        out[var, :, :3 * BLOCK] = np.where(ok, 0.0, NEG_INF)
    out[4, :, :3 * BLOCK] = NEG_INF
    return jnp.asarray(out)


def _attention(z, sink_tab):
    bsz, t, _ = z.shape
    qw = SWA_QH * HEAD_DIM
    nb = t // BLOCK
    cblk = CTX_LEN // BLOCK
    assert nb % ATTN_QB == 0
    at = lambda off: pl.BlockSpec(
        (1, BLOCK, 4 * LANES), lambda b, s: (b, jnp.clip(ATTN_QB * s + off, 0, nb - 1), ODD_KV))
    ctx = pl.BlockSpec((1, CTX_LEN, 4 * LANES), lambda b, s: (b, 0, ODD_KV))

    def variant(i):
        def index(b, s):
            u = ATTN_QB * s + i
            prev_ok = (u - 1 >= cblk).astype(jnp.int32)
            next_ok = (u + 1 <= nb - 1).astype(jnp.int32)
            return (jnp.where(u < cblk, 4, 2 * prev_ok + next_ok), 0, 0)
        return pl.BlockSpec((1, BLOCK, N_KEYS), index)

    bias = _attn_bias()
    return pl.pallas_call(
        _attn_kernel,
        grid=(bsz, nb // ATTN_QB),
        in_specs=[pl.BlockSpec((1, ATTN_QB * BLOCK, qw), lambda b, s: (b, s, ODD_Q)),
                  variant(0), variant(1),
                  pl.BlockSpec(sink_tab.shape, lambda b, s: (0, 0, 0, 0)),
                  at(-1), at(0), at(1), at(2), ctx],
        out_specs=pl.BlockSpec((1, ATTN_QB * BLOCK, qw), lambda b, s: (b, s, 0)),
        out_shape=jax.ShapeDtypeStruct((bsz, t, qw), BF16),
        compiler_params=_params(("arbitrary", "arbitrary")),
        name="swa_attn",
    )(z, bias, bias, sink_tab, z, z, z, z, z)


def _head_norm(o, g_row):
    outs = []
    for hd in range(o.shape[1] // LANES):
        oh = o[:, hd * LANES:(hd + 1) * LANES]
        ms = jnp.mean(oh * oh, axis=-1, keepdims=True)
        outs.append(oh * lax.rsqrt(ms + EPS) * g_row)
    return jnp.concatenate(outs, axis=1)


def _finish(x, gate, y, w_ref, fg_ref, final):
    xn = x + gate * _dot(y.astype(BF16), w_ref[...])
    if final:
        ms = jnp.mean(xn * xn, axis=-1, keepdims=True)
        xn = xn * lax.rsqrt(ms + EPS) * fg_ref[...]
    return xn


def _out_even_kernel(x_ref, mod_ref, of_ref, ob_ref, gg_ref, hf_ref, hb_ref, gl_ref, ng_ref, w_ref, fg_ref,
                     o_ref, *, final, pieces):
    d = D_MODEL
    for sub in range(pieces):
        rows = slice(sub * TM, (sub + 1) * TM)
        ld = lambda r: r[0, rows, :].astype(F32)
        gla = _head_norm(ld(of_ref) + ld(ob_ref), ng_ref[...]) * _silu(ld(gg_ref))
        lru = (ld(hf_ref) + ld(hb_ref)) * _silu(ld(gl_ref))
        gate = _mod_row(mod_ref, sub, final)[:, 2 * d:3 * d]
        o_ref[0, rows, :] = _finish(x_ref[0, rows, :], gate, jnp.concatenate([gla, lru], axis=1),
                                    w_ref, fg_ref, final)


def _out_odd_kernel(x_ref, mod_ref, att_ref, gs_ref, of_ref, ob_ref, gr_ref, ng_ref, w_ref, fg_ref,
                    o_ref, *, final, pieces):
    d = D_MODEL
    for sub in range(pieces):
        rows = slice(sub * TM, (sub + 1) * TM)
        ld = lambda r: r[0, rows, :].astype(F32)
        att = ld(att_ref) * _silu(ld(gs_ref))
        ret = _head_norm(ld(of_ref) + ld(ob_ref), ng_ref[...]) * _silu(ld(gr_ref))
        gate = _mod_row(mod_ref, sub, final)[:, 2 * d:3 * d]
        o_ref[0, rows, :] = _finish(x_ref[0, rows, :], gate, jnp.concatenate([att, ret], axis=1),
                                    w_ref, fg_ref, final)


def _out_proj(body, xs, mod, tok_inputs, ng, w_out, fg, final):
    bsz, t, d = xs.shape
    tp = TM if final else _proj_tile(t)
    skip = 1 if final else 0
    tok = lambda w, c: pl.BlockSpec((1, tp, w), lambda b, j: (b, j + skip, c))
    full = lambda a: pl.BlockSpec(a.shape, lambda b, j: (0,) * a.ndim)
    return pl.pallas_call(
        functools.partial(body, final=final, pieces=tp // TM),
        grid=(bsz, t // tp - skip),
        in_specs=[tok(d, 0), full(mod)] + [tok(V_W, c) for _, c in tok_inputs]
                 + [full(ng), full(w_out), full(fg)],
        out_specs=pl.BlockSpec((1, tp, d), lambda b, j: (b, j, 0)),
        out_shape=jax.ShapeDtypeStruct((bsz, t - skip * TM, d), F32),
        compiler_params=_params(("arbitrary", "arbitrary")),
        name="out_proj",
    )(xs, mod, *[a for a, _ in tok_inputs], ng, w_out, fg)


def _block_diag_gates(wa, wx):
    lb = wa.shape[-1]
    z = jnp.zeros((lb, lb), wa.dtype)

    def pair(w, i):
        return jnp.concatenate([jnp.concatenate([w[2 * i], z], axis=1),
                                jnp.concatenate([z, w[2 * i + 1]], axis=1)], axis=0)

    return jnp.stack([jnp.concatenate([pair(wa, i), pair(wx, i)], axis=1) for i in range(2)]).astype(BF16)


def _rope_tables(n_tok):
    rows = n_tok // GRID_W
    row = jnp.repeat(jnp.arange(rows), GRID_W)
    col = jnp.tile(jnp.arange(GRID_W), rows)
    n_freq = HEAD_DIM // 4
    inv = ROPE_BASE ** (-jnp.arange(n_freq, dtype=F32) / n_freq)
    ang = jnp.concatenate([row.astype(F32)[:, None] * inv[None],
                           col.astype(F32)[:, None] * inv[None]], axis=-1)
    cos, sin = jnp.cos(ang), jnp.sin(ang)
    reps = LANES // HEAD_DIM
    cos_f = jnp.tile(cos, (1, 2 * reps))
    sin_s = jnp.tile(jnp.concatenate([-sin, sin], axis=-1), (1, reps))
    cos_f = jnp.concatenate([jnp.ones((CTX_LEN, LANES), F32), cos_f], axis=0)
    sin_s = jnp.concatenate([jnp.zeros((CTX_LEN, LANES), F32), sin_s], axis=0)
    return cos_f, sin_s


def kernel(x, c, ctx, c_ctx, ada_w, ada_b, norm_g, e_w_in, gla_up_fw, gla_b_fw, gla_up_bw, gla_b_bw, gla_norm_g, lru_conv_w, lru_conv_b, lru_wa_fw, lru_ba_fw, lru_wx_fw, lru_bx_fw, lru_lam_fw, lru_wa_bw, lru_ba_bw, lru_wx_bw, lru_bx_bw, lru_lam_bw, e_w_out, o_w_in, swa_sink, ret_dec_fw, ret_dec_bw, ret_norm_g, o_w_out, final_g):
    bsz, n_tok, d = x.shape
    depth = ada_w.shape[0]
    assert d == D_MODEL and ctx.shape[1] == CTX_LEN and n_tok % TM == 0 and bsz <= CTX_ROW

    cvec = jnp.zeros((SUBLANES, d), F32).at[0:bsz].set(c).at[CTX_ROW].set(c_ctx)
    mods = _ada_mod(cvec, ada_w, ada_b)
    xs = jnp.concatenate([ctx, x], axis=1)
    cos_f, sin_s = _rope_tables(n_tok)
    row = lambda a: a.reshape(1, -1)
    fg = row(final_g)

    for i in range(depth):
        jj = i // 2
        final = i == depth - 1
        mod = mods[i]
        g = row(norm_g[i])
        if i % 2 == 0:
            w = e_w_in[jj]
            w_all = jnp.concatenate(
                [w[:, 0:1536], w[:, 1568:2592], w[:, 1536:1568], jnp.zeros((d, LANES - 2 * GLA_RANK), F32)],
                axis=1).astype(BF16)
            up_pad = jnp.zeros((LANES, 2 * QK_W), F32)
            up_pad = up_pad.at[0:GLA_RANK, 0:QK_W].set(gla_up_fw[jj])
            up_pad = up_pad.at[GLA_RANK:2 * GLA_RANK, QK_W:].set(gla_up_bw[jj]).astype(BF16)
            ub = jnp.concatenate([gla_b_fw[jj], gla_b_bw[jj]]).reshape(1, -1)
            z, la = _proj_even(xs, mod, g, w_all, up_pad, ub)
            o_f, o_b = _bidir_scan(z, la=la)
            h_f, h_b = _lru_scan(
                z, lru_conv_w[jj], row(lru_conv_b[jj]),
                _block_diag_gates(lru_wa_fw[jj], lru_wx_fw[jj]), row(lru_ba_fw[jj]), row(lru_bx_fw[jj]),
                row(lru_lam_fw[jj]),
                _block_diag_gates(lru_wa_bw[jj], lru_wx_bw[jj]), row(lru_ba_bw[jj]), row(lru_bx_bw[jj]),
                row(lru_lam_bw[jj]))
            xs = _out_proj(_out_even_kernel, xs, mod,
                           ((o_f, 0), (o_b, 0), (z, EVEN_GG), (h_f, 0), (h_b, 0), (z, EVEN_GL)),
                           row(gla_norm_g[jj]), e_w_out[jj].astype(BF16), fg, final)
        else:
            z = _proj_odd(xs, mod, g, o_w_in[jj].astype(BF16), cos_f, sin_s)
            sink = swa_sink[jj].reshape(SWA_KVH, 2, 2)
            sink_tab = jnp.broadcast_to(
                jnp.transpose(sink, (0, 2, 1))[:, :, :, None, None],
                (SWA_KVH, 2, 2, BLOCK, LANES)).reshape(SWA_KVH, 2, 2 * BLOCK, LANES)
            att = _attention(z, sink_tab)
            ld_f = row(jnp.repeat(ret_dec_fw[jj], RET_DK))
            ld_b = row(jnp.repeat(ret_dec_bw[jj], RET_DK))
            o_f, o_b = _bidir_scan(z, ld_f=ld_f, ld_b=ld_b)
            xs = _out_proj(_out_odd_kernel, xs, mod,
                           ((att, 0), (z, ODD_GS), (o_f, 0), (o_b, 0), (z, ODD_GR)),
                           row(ret_norm_g[jj]), o_w_out[jj].astype(BF16), fg, final)
    return xs
```

```python
import functools

import numpy as np
import jax
import jax.numpy as jnp
from jax import lax
from jax.experimental import pallas as pl
from jax.experimental.pallas import tpu as pltpu

F32 = jnp.float32
BF16 = jnp.bfloat16

D_MODEL = 1024
CTX_LEN = 256
GRID_W = 64
EPS = 1e-6
NEG_INF = -1e30
ROPE_BASE = 10000.0
CHUNK = 64

GLA_HEADS = 4
GLA_DK = 64
GLA_DV = 128
GLA_RANK = 16
GLA_TEMP = 16.0
LRU_WIDTH = 512
LRU_BLOCKS = 4
LRU_CONV = 4
LRU_C = 8.0
HEAD_DIM = 64
SWA_QH = 8
SWA_KVH = 2
WINDOW = 128
BLOCK = 128
RET_HEADS = 4
RET_DK = 64
RET_DV = 128

LANES = 128
SUBLANES = 8
PACKED_ROWS = 16
TM = 256
CTX_ROW = SUBLANES - 1
QK_W = GLA_HEADS * GLA_DK
V_W = GLA_HEADS * GLA_DV
QKV_W = 2 * QK_W + V_W
VMEM_LIMIT = 56 * 1024 * 1024

EVEN_GG, EVEN_XR, EVEN_GL, EVEN_COLS = 2, 3, 4, 2560
ODD_Q, ODD_KV, ODD_GS, ODD_GR, ODD_COLS = 2, 3, 4, 5, 3072


def _params(sem):
    return pltpu.CompilerParams(dimension_semantics=sem, vmem_limit_bytes=VMEM_LIMIT)


def _dot(a, b):
    return jnp.dot(a, b, preferred_element_type=F32)


def _dot_nt(a, b):
    return lax.dot_general(a, b, (((1,), (1,)), ((), ())), preferred_element_type=F32)


def _dot_tn(a, b):
    return lax.dot_general(a, b, (((0,), (0,)), ((), ())), preferred_element_type=F32)


def _sigmoid(x):
    return 1.0 / (1.0 + jnp.exp(-x))


def _silu(x):
    return x * _sigmoid(x)


def _log_sigmoid(x):
    return jnp.minimum(x, 0.0) - jnp.log(1.0 + jnp.exp(-jnp.abs(x)))


def _proj_tile(t):
    n = t // TM
    for k in (5, 4, 3, 2, 1):
        if n % k == 0:
            return k * TM
    return TM


def _ada_kernel(c_ref, w_ref, b_ref, o_ref):
    s = _silu(c_ref[...])
    o_ref[0] = _dot(s.astype(BF16), w_ref[0].astype(BF16)) + b_ref[0]


def _ada_mod(cvec, ada_w, ada_b):
    depth, d, d3 = ada_w.shape
    nsplit = d3 // d
    return pl.pallas_call(
        _ada_kernel,
        grid=(depth, nsplit),
        in_specs=[
            pl.BlockSpec((SUBLANES, d), lambda i, n: (0, 0)),
            pl.BlockSpec((1, d, d), lambda i, n: (i, 0, n)),
            pl.BlockSpec((1, 1, d), lambda i, n: (i, 0, n)),
        ],
        out_specs=pl.BlockSpec((1, SUBLANES, d), lambda i, n: (i, 0, n)),
        out_shape=jax.ShapeDtypeStruct((depth, SUBLANES, d3), F32),
        compiler_params=_params(("arbitrary", "arbitrary")),
        name="ada_mod",
    )(cvec, ada_w, ada_b.reshape(depth, 1, d3))


def _mod_row(mod_ref, sub, skip):
    if skip:
        return mod_ref[pl.ds(pl.program_id(0), 1), :]
    is_ctx = jnp.logical_and(pl.program_id(1) == 0, sub == 0)
    return mod_ref[pl.ds(jnp.where(is_ctx, CTX_ROW, pl.program_id(0)), 1), :]


def _modulated(x, m, g_row):
    d = D_MODEL
    ms = jnp.mean(x * x, axis=-1, keepdims=True)
    return (x * lax.rsqrt(ms + EPS)) * g_row * (1.0 + m[:, d:2 * d]) + m[:, 0:d]


def _proj_even_kernel(x_ref, mod_ref, g_ref, w_ref, up_ref, ub_ref, z_ref, la_ref, *, pieces):
    for sub in range(pieces):
        rows = slice(sub * TM, (sub + 1) * TM)
        h = _modulated(x_ref[0, rows, :], _mod_row(mod_ref, sub, False), g_ref[...])
        z = _dot(h.astype(BF16), w_ref[...])
        z_ref[0, rows, 0:QK_W] = (z[:, 0:QK_W] * (GLA_DK ** -0.5)).astype(BF16)
        z_ref[0, rows, QK_W:EVEN_COLS] = z[:, QK_W:EVEN_COLS].astype(BF16)
        pre = _dot(z[:, EVEN_COLS:EVEN_COLS + LANES].astype(BF16), up_ref[...]) + ub_ref[...]
        la_ref[0, rows, :] = _log_sigmoid(pre) * (1.0 / GLA_TEMP)


def _proj_even(xs, mod, g, w_all, up_pad, ub):
    bsz, t, d = xs.shape
    tp = _proj_tile(t)
    tok = lambda w: pl.BlockSpec((1, tp, w), lambda b, j: (b, j, 0))
    full = lambda a: pl.BlockSpec(a.shape, lambda b, j: (0,) * a.ndim)
    return pl.pallas_call(
        functools.partial(_proj_even_kernel, pieces=tp // TM),
        grid=(bsz, t // tp),
        in_specs=[tok(d), full(mod), full(g), full(w_all), full(up_pad), full(ub)],
        out_specs=[tok(EVEN_COLS), tok(2 * QK_W)],
        out_shape=[jax.ShapeDtypeStruct((bsz, t, EVEN_COLS), BF16),
                   jax.ShapeDtypeStruct((bsz, t, 2 * QK_W), F32)],
        compiler_params=_params(("arbitrary", "arbitrary")),
        name="proj_even",
    )(xs, mod, g, w_all, up_pad, ub)


def _rope(x, cos_f, sin_s, lane):
    outs = []
    for g in range(x.shape[1] // LANES):
        xg = x[:, g * LANES:(g + 1) * LANES]
        up = pltpu.roll(xg, LANES - HEAD_DIM // 2, axis=1)
        dn = pltpu.roll(xg, HEAD_DIM // 2, axis=1)
        swapped = jnp.where(lane % HEAD_DIM < HEAD_DIM // 2, up, dn)
        outs.append(xg * cos_f + swapped * sin_s)
    return outs[0] if len(outs) == 1 else jnp.concatenate(outs, axis=1)


def _proj_odd_kernel(x_ref, mod_ref, g_ref, w_ref, cos_ref, sin_ref, z_ref, *, pieces):
    lane = lax.broadcasted_iota(jnp.int32, (TM, LANES), 1)
    for sub in range(pieces):
        rows = slice(sub * TM, (sub + 1) * TM)
        h = _modulated(x_ref[0, rows, :], _mod_row(mod_ref, sub, False), g_ref[...])
        z = _dot(h.astype(BF16), w_ref[...])
        cos_f = cos_ref[rows, :]
        sin_s = sin_ref[rows, :]
        rope = lambda a: _rope(a, cos_f, sin_s, lane)
        k = rope(z[:, 512:640])
        v = z[:, 640:768]
        c0 = ODD_Q * 512
        z_ref[0, rows, 0:QK_W] = rope(z[:, 1280:1536]).astype(BF16)
        z_ref[0, rows, QK_W:2 * QK_W] = (rope(z[:, 1536:1792]) * (RET_DK ** -0.5)).astype(BF16)
        z_ref[0, rows, 2 * QK_W:QKV_W] = z[:, 1792:2304].astype(BF16)
        z_ref[0, rows, c0:c0 + 512] = (rope(z[:, 0:512]) * (HEAD_DIM ** -0.5)).astype(BF16)
        z_ref[0, rows, c0 + 512:c0 + 640] = k.astype(BF16)
        z_ref[0, rows, c0 + 640:c0 + 768] = pltpu.roll(k, HEAD_DIM, axis=1).astype(BF16)
        z_ref[0, rows, c0 + 768:c0 + 896] = v.astype(BF16)
        z_ref[0, rows, c0 + 896:c0 + 1024] = pltpu.roll(v, HEAD_DIM, axis=1).astype(BF16)
        z_ref[0, rows, ODD_GS * 512:(ODD_GS + 1) * 512] = z[:, 768:1280].astype(BF16)
        z_ref[0, rows, ODD_GR * 512:(ODD_GR + 1) * 512] = z[:, 2304:2816].astype(BF16)


def _proj_odd(xs, mod, g, w_in, cos_f, sin_s):
    bsz, t, d = xs.shape
    tp = _proj_tile(t)
    tok = lambda w: pl.BlockSpec((1, tp, w), lambda b, j: (b, j, 0))
    full = lambda a: pl.BlockSpec(a.shape, lambda b, j: (0,) * a.ndim)
    tab = pl.BlockSpec((tp, LANES), lambda b, j: (j, 0))
    return pl.pallas_call(
        functools.partial(_proj_odd_kernel, pieces=tp // TM),
        grid=(bsz, t // tp),
        in_specs=[tok(d), full(mod), full(g), full(w_in), tab, tab],
        out_specs=tok(ODD_COLS),
        out_shape=jax.ShapeDtypeStruct((bsz, t, ODD_COLS), BF16),
        compiler_params=_params(("arbitrary", "arbitrary")),
        name="proj_odd",
    )(xs, mod, g, w_in, cos_f, sin_s)


def _bwd_tile(j, nt):
    return jnp.where(j == 0, 0, nt - j)


def _scan_both_directions(zf, bcum_f, zb, bcum_b, sf_ref, sb_ref, of_ref, ob_ref):
    nch = TM // CHUNK
    col_k = lax.broadcasted_iota(jnp.int32, (CHUNK, QK_W), 1)
    row_k = lax.broadcasted_iota(jnp.int32, (CHUNK, QK_W), 0)
    col_v = lax.broadcasted_iota(jnp.int32, (CHUNK, V_W), 1)
    tris = ((col_k % CHUNK) <= row_k, (col_k % CHUNK) >= row_k)
    bd = (lax.broadcasted_iota(jnp.int32, (V_W, QK_W), 0) // GLA_DV
          == lax.broadcasted_iota(jnp.int32, (V_W, QK_W), 1) // GLA_DK)
    dirs = ((zf, bcum_f, sf_ref, of_ref, False), (zb, bcum_b, sb_ref, ob_ref, True))
    preps, all_atts = [], []
    for step in range(nch):
        prep = []
        for qkv, bcum, _, _, reverse in dirs:
            r0 = (nch - 1 - step if reverse else step) * CHUNK
            b = bcum[r0:r0 + CHUNK]
            last = b[0:1] if reverse else b[CHUNK - 1:CHUNK]
            qc = qkv[r0:r0 + CHUNK, 0:QK_W].astype(F32)
            kc = qkv[r0:r0 + CHUNK, QK_W:2 * QK_W].astype(F32)
            vc = qkv[r0:r0 + CHUNK, 2 * QK_W:QKV_W]
            q_in = (qc * jnp.exp(b)).astype(BF16)
            k_in = kc * jnp.exp(-b)
            k_st = (kc * jnp.exp(last - b)).astype(BF16)
            k_bd = jnp.concatenate(
                [jnp.where(col_k // GLA_DK == h, k_in, 0.0) for h in range(GLA_HEADS)], axis=0).astype(BF16)
            v_bd = jnp.concatenate(
                [jnp.where(col_v // GLA_DV == h, vc, jnp.zeros_like(vc)) for h in range(GLA_HEADS)], axis=0)
            prep.append((r0, q_in, k_bd, v_bd, vc, k_st, jnp.exp(last)))
        preps.append(prep)
        all_atts.append([_dot_nt(q_in, k_bd) for _, q_in, k_bd, _, _, _, _ in prep])
    for prep, atts in zip(preps, all_atts):
        d_sts = [_dot_tn(vc, k_st) for _, _, _, _, vc, k_st, _ in prep]
        for (r0, q_in, _, v_bd, _, _, decay), att, d_st, tri, (_, _, s_ref, o_ref, _) in zip(
                prep, atts, d_sts, tris, dirs):
            s_t = s_ref[...]
            o = _dot(jnp.where(tri, att, 0.0).astype(BF16), v_bd) + _dot_nt(q_in, s_t.astype(BF16))
            o_ref[0, r0:r0 + CHUNK, :] = o.astype(o_ref.dtype)
            s_ref[...] = s_t * decay + jnp.where(bd, d_st, 0.0)


def _cumsum_chunks(la, tri_ref):
    hi = la.astype(BF16)
    lo = (la - hi.astype(F32)).astype(BF16)
    return _dot(tri_ref[...], hi) + _dot(tri_ref[...], lo)


def _gla_kernel(zf, laf, zb, lab, trif, trib, of_ref, ob_ref, sf_ref, sb_ref):
    @pl.when(pl.program_id(1) == 0)
    def _():
        sf_ref[...] = jnp.zeros_like(sf_ref)
        sb_ref[...] = jnp.zeros_like(sb_ref)

    _scan_both_directions(zf[0], _cumsum_chunks(laf[0], trif), zb[0], _cumsum_chunks(lab[0], trib),
                          sf_ref, sb_ref, of_ref, ob_ref)


def _ret_kernel(zf, zb, ldf, ldb, of_ref, ob_ref, sf_ref, sb_ref):
    @pl.when(pl.program_id(1) == 0)
    def _():
        sf_ref[...] = jnp.zeros_like(sf_ref)
        sb_ref[...] = jnp.zeros_like(sb_ref)

    pos = lax.broadcasted_iota(jnp.int32, (TM, QK_W), 0) % CHUNK
    bf = (pos + 1).astype(F32) * _log_sigmoid(ldf[...])
    bb = (CHUNK - pos).astype(F32) * _log_sigmoid(ldb[...])
    _scan_both_directions(zf[0], bf, zb[0], bb, sf_ref, sb_ref, of_ref, ob_ref)


def _chunk_tri(reverse):
    t = np.arange(TM)
    same = (t[:, None] // CHUNK) == (t[None, :] // CHUNK)
    order = (t[None, :] >= t[:, None]) if reverse else (t[None, :] <= t[:, None])
    return jnp.asarray(same & order, dtype=BF16)


def _bidir_scan(z, la=None, ld_f=None, ld_b=None):
    bsz, t, _ = z.shape
    nt = t // TM
    fwd = lambda w, c: pl.BlockSpec((1, TM, w), lambda b, j: (b, j, c))
    bwd = lambda w, c: pl.BlockSpec((1, TM, w), lambda b, j: (b, _bwd_tile(j, nt), c))
    full = lambda a: pl.BlockSpec(a.shape, lambda b, j: (0,) * a.ndim)
    common = dict(
        grid=(bsz, nt),
        out_specs=[fwd(V_W, 0), bwd(V_W, 0)],
        out_shape=[jax.ShapeDtypeStruct((bsz, t, V_W), BF16)] * 2,
        scratch_shapes=[pltpu.VMEM((V_W, QK_W), F32)] * 2,
        compiler_params=_params(("arbitrary", "arbitrary")),
    )
    if la is not None:
        tri_f, tri_b = _chunk_tri(False), _chunk_tri(True)
        return pl.pallas_call(
            _gla_kernel,
            in_specs=[fwd(QKV_W, 0), fwd(QK_W, 0), bwd(QKV_W, 0), bwd(QK_W, 1), full(tri_f), full(tri_b)],
            name="gla_scan", **common,
        )(z, la, z, la, tri_f, tri_b)
    return pl.pallas_call(
        _ret_kernel,
        in_specs=[fwd(QKV_W, 0), bwd(QKV_W, 0), full(ld_f), full(ld_b)],
        name="ret_scan", **common,
    )(z, z, ld_f, ld_b)


def _lru_conv(x_ref, prev_ref, next_ref, cw_ref, cb_ref, tile, nt):
    prev_ok = tile >= 2
    next_ok = jnp.logical_and(tile >= 1, tile <= nt - 2)
    prev = jnp.where(prev_ok, prev_ref[0].astype(F32), 0.0)
    nxt = jnp.where(next_ok, next_ref[0].astype(F32), 0.0)
    ext = jnp.concatenate([prev, x_ref[0].astype(F32), nxt], axis=0)
    n = ext.shape[0]
    lo, hi = PACKED_ROWS, PACKED_ROWS + TM
    cw = cw_ref[...]
    acc = ext[lo:hi] * cw[2:3]
    acc += pltpu.roll(ext, 2, axis=0)[lo:hi] * cw[0:1]
    acc += pltpu.roll(ext, 1, axis=0)[lo:hi] * cw[1:2]
    acc += pltpu.roll(ext, n - 1, axis=0)[lo:hi] * cw[3:4]
    return acc + cb_ref[...]


def _lru_one_direction(xc, w_ref, ba_ref, bx_ref, lam_ref, carry_ref, o_ref, reverse):
    half = LRU_WIDTH // 2
    xcb = xc.astype(BF16)
    pre0 = _dot(xcb[:, 0:half], w_ref[0])
    pre1 = _dot(xcb[:, half:], w_ref[1])
    pa = jnp.concatenate([pre0[:, 0:half], pre1[:, 0:half]], axis=1) + ba_ref[...]
    px = jnp.concatenate([pre0[:, half:], pre1[:, half:]], axis=1) + bx_ref[...]
    r = _sigmoid(pa)
    ig = _sigmoid(px)
    log_a = (LRU_C * r) * _log_sigmoid(lam_ref[...])
    a = jnp.exp(log_a)
    u = jnp.sqrt(1.0 - a * a) * (ig * xc)

    ng = TM // SUBLANES
    a3 = a.reshape(ng, SUBLANES, LRU_WIDTH)
    u3 = u.reshape(ng, SUBLANES, LRU_WIDTH)
    sub = lax.broadcasted_iota(jnp.int32, (ng, SUBLANES, LRU_WIDTH), 1)
    k = 1
    while k < SUBLANES:
        if reverse:
            a_s = pltpu.roll(a3, SUBLANES - k, axis=1)
            u_s = pltpu.roll(u3, SUBLANES - k, axis=1)
            ok = sub < SUBLANES - k
        else:
            a_s = pltpu.roll(a3, k, axis=1)
            u_s = pltpu.roll(u3, k, axis=1)
            ok = sub >= k
        u3 = a3 * jnp.where(ok, u_s, 0.0) + u3
        a3 = a3 * jnp.where(ok, a_s, 1.0)
        k *= 2

    edge = 0 if reverse else SUBLANES - 1
    carry = carry_ref[...]
    starts = [None] * ng
    for g in (range(ng - 1, -1, -1) if reverse else range(ng)):
        starts[g] = carry
        carry = a3[g, edge:edge + 1, :] * carry + u3[g, edge:edge + 1, :]
    carry_ref[...] = carry
    h = jnp.concatenate([a3[g] * starts[g] + u3[g] for g in range(ng)], axis=0)
    o_ref[0] = h.astype(o_ref.dtype)


def _lru_kernel(xf, xfp, xfn, xb, xbp, xbn, cw, cb,
                wf, baf, bxf, lamf, wb, bab, bxb, lamb,
                hf_ref, hb_ref, cf_ref, cbk_ref, *, nt):
    j = pl.program_id(1)

    @pl.when(j == 0)
    def _():
        cf_ref[...] = jnp.zeros_like(cf_ref)
        cbk_ref[...] = jnp.zeros_like(cbk_ref)

    xc_f = _lru_conv(xf, xfp, xfn, cw, cb, j, nt)
    _lru_one_direction(xc_f, wf, baf, bxf, lamf, cf_ref, hf_ref, False)
    xc_b = _lru_conv(xb, xbp, xbn, cw, cb, _bwd_tile(j, nt), nt)
    _lru_one_direction(xc_b, wb, bab, bxb, lamb, cbk_ref, hb_ref, True)


def _lru_scan(z, cw, cb, wf, baf, bxf, lamf, wb, bab, bxb, lamb):
    bsz, t, _ = z.shape
    w = LRU_WIDTH
    nt = t // TM
    per = TM // PACKED_ROWS
    nrow = t // PACKED_ROWS
    f_t = lambda j: j
    b_t = lambda j: _bwd_tile(j, nt)

    def specs(tile_of):
        return [
            pl.BlockSpec((1, TM, w), lambda b, j: (b, tile_of(j), EVEN_XR)),
            pl.BlockSpec((1, PACKED_ROWS, w),
                         lambda b, j: (b, jnp.maximum(tile_of(j) * per - 1, 0), EVEN_XR)),
            pl.BlockSpec((1, PACKED_ROWS, w),
                         lambda b, j: (b, jnp.minimum((tile_of(j) + 1) * per, nrow - 1), EVEN_XR)),
        ]

    full = lambda a: pl.BlockSpec(a.shape, lambda b, j: (0,) * a.ndim)
    consts = (cw, cb, wf, baf, bxf, lamf, wb, bab, bxb, lamb)
    return pl.pallas_call(
        functools.partial(_lru_kernel, nt=nt),
        grid=(bsz, nt),
        in_specs=specs(f_t) + specs(b_t) + [full(a) for a in consts],
        out_specs=[pl.BlockSpec((1, TM, w), lambda b, j: (b, j, 0)),
                   pl.BlockSpec((1, TM, w), lambda b, j: (b, b_t(j), 0))],
        out_shape=[jax.ShapeDtypeStruct((bsz, t, w), BF16)] * 2,
        scratch_shapes=[pltpu.VMEM((1, w), F32)] * 2,
        compiler_params=_params(("arbitrary", "arbitrary")),
        name="lru_scan",
    )(z, z, z, z, z, z, *consts)


N_KEYS = 3 * BLOCK + CTX_LEN
ATTN_QB = 5


def _attn_kernel(q_ref, sink_ref, *refs):
    bias_refs = refs[0:ATTN_QB]
    blocks = [r[0] for r in refs[ATTN_QB:2 * ATTN_QB + 2]]
    ctx = refs[2 * ATTN_QB + 2][0]
    o_ref = refs[2 * ATTN_QB + 3]
    low = lax.broadcasted_iota(jnp.int32, (2 * BLOCK, LANES), 1) < HEAD_DIM
    keep_lo = jnp.where(low, 1.0, 0.0).astype(BF16)
    keep_hi = jnp.where(low, 0.0, 1.0).astype(BF16)
    group = (SWA_QH // SWA_KVH) * HEAD_DIM
    sinks = [sink_ref[h, par][:, 0:1] for h in range(SWA_KVH) for par in range(2)] * ATTN_QB

    scores, vals = [], []
    for i, bias_ref in enumerate(bias_refs):
        kv = jnp.concatenate(blocks[i:i + 3] + [ctx], axis=0)
        kk, kr, vv, vr = (kv[:, n * LANES:(n + 1) * LANES] for n in range(4))
        bias = bias_ref[0]
        bias2 = jnp.concatenate([bias, bias], axis=0)
        q = q_ref[0, i * BLOCK:(i + 1) * BLOCK, :]
        for h in range(SWA_KVH):
            q2 = jnp.concatenate([q[:, h * group:h * group + LANES],
                                  q[:, h * group + LANES:(h + 1) * group]], axis=0)
            k_lo, k_hi = (kk, kr) if h == 0 else (kr, kk)
            scores.append(_dot_nt(q2 * keep_lo, k_lo) + bias2)
            scores.append(_dot_nt(q2 * keep_hi, k_hi) + bias2)
        vals += [vv, vr, vr, vv]

    maxes = [jnp.maximum(jnp.max(s, axis=-1, keepdims=True), sk) for s, sk in zip(scores, sinks)]
    probs = [jnp.exp(s - m) for s, m in zip(scores, maxes)]
    dens = [jnp.sum(p, axis=-1, keepdims=True) + jnp.exp(sk - m) for p, sk, m in zip(probs, sinks, maxes)]
    outs = [_dot(p.astype(BF16), v_sel) / den for p, v_sel, den in zip(probs, vals, dens)]

    for i in range(ATTN_QB):
        for h in range(SWA_KVH):
            n = 2 * (SWA_KVH * i + h)
            o2 = jnp.where(low, outs[n], outs[n + 1]).astype(o_ref.dtype)
            o_ref[0, i * BLOCK:(i + 1) * BLOCK, h * group:h * group + LANES] = o2[0:BLOCK]
            o_ref[0, i * BLOCK:(i + 1) * BLOCK, h * group + LANES:(h + 1) * group] = o2[BLOCK:2 * BLOCK]


def _attn_bias():
    qi = np.arange(BLOCK)[:, None]
    kj = np.arange(3 * BLOCK)[None, :]
    near = np.abs(kj - BLOCK - qi) <= WINDOW
    blk = kj // BLOCK
    out = np.zeros((5, BLOCK, N_KEYS), np.float32)
    for var in range(4):
        ok = near & ((blk != 0) | bool(var & 2)) & ((blk != 2) | bool(var & 1))
        out[var, :, :3 * BLOCK] = np.where(ok, 0.0, NEG_INF)
    out[4, :, :3 * BLOCK] = NEG_INF
    return jnp.asarray(out)


def _attention(z, sink_tab):
    bsz, t, _ = z.shape
    qw = SWA_QH * HEAD_DIM
    nb = t // BLOCK
    cblk = CTX_LEN // BLOCK
    assert nb % ATTN_QB == 0
    at = lambda off: pl.BlockSpec(
        (1, BLOCK, 4 * LANES), lambda b, s: (b, jnp.clip(ATTN_QB * s + off, 0, nb - 1), ODD_KV))
    ctx = pl.BlockSpec((1, CTX_LEN, 4 * LANES), lambda b, s: (b, 0, ODD_KV))

    def variant(i):
        def index(b, s):
            u = ATTN_QB * s + i
            prev_ok = (u - 1 >= cblk).astype(jnp.int32)
            next_ok = (u + 1 <= nb - 1).astype(jnp.int32)
            return (jnp.where(u < cblk, 4, 2 * prev_ok + next_ok), 0, 0)
        return pl.BlockSpec((1, BLOCK, N_KEYS), index)

    bias = _attn_bias()
    return pl.pallas_call(
        _attn_kernel,
        grid=(bsz, nb // ATTN_QB),
        in_specs=[pl.BlockSpec((1, ATTN_QB * BLOCK, qw), lambda b, s: (b, s, ODD_Q)),
                  pl.BlockSpec(sink_tab.shape, lambda b, s: (0, 0, 0, 0))]
                 + [variant(i) for i in range(ATTN_QB)]
                 + [at(off) for off in range(-1, ATTN_QB + 1)] + [ctx],
        out_specs=pl.BlockSpec((1, ATTN_QB * BLOCK, qw), lambda b, s: (b, s, 0)),
        out_shape=jax.ShapeDtypeStruct((bsz, t, qw), BF16),
        compiler_params=_params(("arbitrary", "arbitrary")),
        name="swa_attn",
    )(z, sink_tab, *([bias] * ATTN_QB + [z] * (ATTN_QB + 3)))


def _head_norm(o, g_row):
    outs = []
    for hd in range(o.shape[1] // LANES):
        oh = o[:, hd * LANES:(hd + 1) * LANES]
        ms = jnp.mean(oh * oh, axis=-1, keepdims=True)
        outs.append(oh * lax.rsqrt(ms + EPS) * g_row)
    return jnp.concatenate(outs, axis=1)


def _finish(x, gate, y, w_ref, fg_ref, final):
    xn = x + gate * _dot(y.astype(BF16), w_ref[...])
    if final:
        ms = jnp.mean(xn * xn, axis=-1, keepdims=True)
        xn = xn * lax.rsqrt(ms + EPS) * fg_ref[...]
    return xn


def _out_even_kernel(x_ref, mod_ref, of_ref, ob_ref, gg_ref, hf_ref, hb_ref, gl_ref, ng_ref, w_ref, fg_ref,
                     o_ref, *, final, pieces):
    d = D_MODEL
    for sub in range(pieces):
        rows = slice(sub * TM, (sub + 1) * TM)
        ld = lambda r: r[0, rows, :].astype(F32)
        gla = _head_norm(ld(of_ref) + ld(ob_ref), ng_ref[...]) * _silu(ld(gg_ref))
        lru = (ld(hf_ref) + ld(hb_ref)) * _silu(ld(gl_ref))
        gate = _mod_row(mod_ref, sub, final)[:, 2 * d:3 * d]
        o_ref[0, rows, :] = _finish(x_ref[0, rows, :], gate, jnp.concatenate([gla, lru], axis=1),
                                    w_ref, fg_ref, final)


def _out_odd_kernel(x_ref, mod_ref, att_ref, gs_ref, of_ref, ob_ref, gr_ref, ng_ref, w_ref, fg_ref,
                    o_ref, *, final, pieces):
    d = D_MODEL
    for sub in range(pieces):
        rows = slice(sub * TM, (sub + 1) * TM)
        ld = lambda r: r[0, rows, :].astype(F32)
        att = ld(att_ref) * _silu(ld(gs_ref))
        ret = _head_norm(ld(of_ref) + ld(ob_ref), ng_ref[...]) * _silu(ld(gr_ref))
        gate = _mod_row(mod_ref, sub, final)[:, 2 * d:3 * d]
        o_ref[0, rows, :] = _finish(x_ref[0, rows, :], gate, jnp.concatenate([att, ret], axis=1),
                                    w_ref, fg_ref, final)


def _out_proj(body, xs, mod, tok_inputs, ng, w_out, fg, final):
    bsz, t, d = xs.shape
    tp = TM if final else _proj_tile(t)
    skip = 1 if final else 0
    tok = lambda w, c: pl.BlockSpec((1, tp, w), lambda b, j: (b, j + skip, c))
    full = lambda a: pl.BlockSpec(a.shape, lambda b, j: (0,) * a.ndim)
    return pl.pallas_call(
        functools.partial(body, final=final, pieces=tp // TM),
        grid=(bsz, t // tp - skip),
        in_specs=[tok(d, 0), full(mod)] + [tok(V_W, c) for _, c in tok_inputs]
                 + [full(ng), full(w_out), full(fg)],
        out_specs=pl.BlockSpec((1, tp, d), lambda b, j: (b, j, 0)),
        out_shape=jax.ShapeDtypeStruct((bsz, t - skip * TM, d), F32),
        compiler_params=_params(("arbitrary", "arbitrary")),
        name="out_proj",
    )(xs, mod, *[a for a, _ in tok_inputs], ng, w_out, fg)


def _block_diag_gates(wa, wx):
    lb = wa.shape[-1]
    z = jnp.zeros((lb, lb), wa.dtype)

    def pair(w, i):
        return jnp.concatenate([jnp.concatenate([w[2 * i], z], axis=1),
                                jnp.concatenate([z, w[2 * i + 1]], axis=1)], axis=0)

    return jnp.stack([jnp.concatenate([pair(wa, i), pair(wx, i)], axis=1) for i in range(2)]).astype(BF16)


def _rope_tables(n_tok):
    rows = n_tok // GRID_W
    row = jnp.repeat(jnp.arange(rows), GRID_W)
    col = jnp.tile(jnp.arange(GRID_W), rows)
    n_freq = HEAD_DIM // 4
    inv = ROPE_BASE ** (-jnp.arange(n_freq, dtype=F32) / n_freq)
    ang = jnp.concatenate([row.astype(F32)[:, None] * inv[None],
                           col.astype(F32)[:, None] * inv[None]], axis=-1)
    cos, sin = jnp.cos(ang), jnp.sin(ang)
    reps = LANES // HEAD_DIM
    cos_f = jnp.tile(cos, (1, 2 * reps))
    sin_s = jnp.tile(jnp.concatenate([-sin, sin], axis=-1), (1, reps))
    cos_f = jnp.concatenate([jnp.ones((CTX_LEN, LANES), F32), cos_f], axis=0)
    sin_s = jnp.concatenate([jnp.zeros((CTX_LEN, LANES), F32), sin_s], axis=0)
    return cos_f, sin_s


def kernel(x, c, ctx, c_ctx, ada_w, ada_b, norm_g, e_w_in, gla_up_fw, gla_b_fw, gla_up_bw, gla_b_bw, gla_norm_g, lru_conv_w, lru_conv_b, lru_wa_fw, lru_ba_fw, lru_wx_fw, lru_bx_fw, lru_lam_fw, lru_wa_bw, lru_ba_bw, lru_wx_bw, lru_bx_bw, lru_lam_bw, e_w_out, o_w_in, swa_sink, ret_dec_fw, ret_dec_bw, ret_norm_g, o_w_out, final_g):
    bsz, n_tok, d = x.shape
    depth = ada_w.shape[0]
    assert d == D_MODEL and ctx.shape[1] == CTX_LEN and n_tok % TM == 0 and bsz <= CTX_ROW

    cvec = jnp.zeros((SUBLANES, d), F32).at[0:bsz].set(c).at[CTX_ROW].set(c_ctx)
    mods = _ada_mod(cvec, ada_w, ada_b)
    xs = jnp.concatenate([ctx, x], axis=1)
    cos_f, sin_s = _rope_tables(n_tok)
    row = lambda a: a.reshape(1, -1)
    fg = row(final_g)

    for i in range(depth):
        jj = i // 2
        final = i == depth - 1
        mod = mods[i]
        g = row(norm_g[i])
        if i % 2 == 0:
            w = e_w_in[jj]
            w_all = jnp.concatenate(
                [w[:, 0:1536], w[:, 1568:2592], w[:, 1536:1568], jnp.zeros((d, LANES - 2 * GLA_RANK), F32)],
                axis=1).astype(BF16)
            up_pad = jnp.zeros((LANES, 2 * QK_W), F32)
            up_pad = up_pad.at[0:GLA_RANK, 0:QK_W].set(gla_up_fw[jj])
            up_pad = up_pad.at[GLA_RANK:2 * GLA_RANK, QK_W:].set(gla_up_bw[jj]).astype(BF16)
            ub = jnp.concatenate([gla_b_fw[jj], gla_b_bw[jj]]).reshape(1, -1)
            z, la = _proj_even(xs, mod, g, w_all, up_pad, ub)
            o_f, o_b = _bidir_scan(z, la=la)
            h_f, h_b = _lru_scan(
                z, lru_conv_w[jj], row(lru_conv_b[jj]),
                _block_diag_gates(lru_wa_fw[jj], lru_wx_fw[jj]), row(lru_ba_fw[jj]), row(lru_bx_fw[jj]),
                row(lru_lam_fw[jj]),
                _block_diag_gates(lru_wa_bw[jj], lru_wx_bw[jj]), row(lru_ba_bw[jj]), row(lru_bx_bw[jj]),
                row(lru_lam_bw[jj]))
            xs = _out_proj(_out_even_kernel, xs, mod,
                           ((o_f, 0), (o_b, 0), (z, EVEN_GG), (h_f, 0), (h_b, 0), (z, EVEN_GL)),
                           row(gla_norm_g[jj]), e_w_out[jj].astype(BF16), fg, final)
        else:
            z = _proj_odd(xs, mod, g, o_w_in[jj].astype(BF16), cos_f, sin_s)
            sink = swa_sink[jj].reshape(SWA_KVH, 2, 2)
            sink_tab = jnp.broadcast_to(
                jnp.transpose(sink, (0, 2, 1))[:, :, :, None, None],
                (SWA_KVH, 2, 2, BLOCK, LANES)).reshape(SWA_KVH, 2, 2 * BLOCK, LANES)
            att = _attention(z, sink_tab)
            ld_f = row(jnp.repeat(ret_dec_fw[jj], RET_DK))
            ld_b = row(jnp.repeat(ret_dec_bw[jj], RET_DK))
            o_f, o_b = _bidir_scan(z, ld_f=ld_f, ld_b=ld_b)
            xs = _out_proj(_out_odd_kernel, xs, mod,
                           ((att, 0), (z, ODD_GS), (o_f, 0), (o_b, 0), (z, ODD_GR)),
                           row(ret_norm_g[jj]), o_w_out[jj].astype(BF16), fg, final)
    return xs
```
